```python
import jax, jax.numpy as jnp
from jax import lax
import numpy as np

D_MODEL = 2048
BATCH = 2
SEQ = 4096
DEPTH = 1

SG_GROUPS = 8
SG_GROUP_DIM = 128
SG_WIDTH = SG_GROUPS * SG_GROUP_DIM
SG_CHUNK = 128
N_HEADS = 8
N_KV_HEADS = 2
HEAD_DIM = 128
Q_PER_KV = N_HEADS // N_KV_HEADS
Q_WIDTH = N_HEADS * HEAD_DIM
KV_WIDTH = N_KV_HEADS * HEAD_DIM
WINDOW = 128
BLOCK = 128
ROPE_THETA = 500000.0
ROT_DIM = HEAD_DIM // 4
N_BRANCHES = 2
D_FF = 5632
CONV_WIDTH = 3
EPS = 1e-6
MASK_VALUE = -1e30

IN_WIDTH = 2 * SG_WIDTH + Q_WIDTH + 2 * KV_WIDTH + N_BRANCHES * D_MODEL

kernel_name = "hybrid_sgmlp_swa_sink_convffn_block"


def rms_norm(x, g):
    xf = x.astype(jnp.float32)
    y = xf * lax.rsqrt(jnp.mean(xf * xf, axis=-1, keepdims=True) + EPS)
    return (y * g.astype(jnp.float32)).astype(x.dtype)


def layer_norm(x, g, b):
    xf = x.astype(jnp.float32)
    mu = jnp.mean(xf, axis=-1, keepdims=True)
    xc = xf - mu
    y = xc * lax.rsqrt(jnp.mean(xc * xc, axis=-1, keepdims=True) + EPS)
    return (y * g.astype(jnp.float32) + b.astype(jnp.float32)).astype(x.dtype)


def partial_rope(x):
    s = x.shape[1]
    pos = jnp.arange(s, dtype=jnp.float32)
    inv_freq = ROPE_THETA ** (-jnp.arange(0, ROT_DIM, 2, dtype=jnp.float32) / ROT_DIM)
    ang = pos[:, None] * inv_freq[None, :]
    cos = jnp.cos(ang)[None, :, None, :]
    sin = jnp.sin(ang)[None, :, None, :]
    xf = x.astype(jnp.float32)
    x1 = xf[..., : ROT_DIM // 2]
    x2 = xf[..., ROT_DIM // 2: ROT_DIM]
    rot = jnp.concatenate([x1 * cos - x2 * sin, x2 * cos + x1 * sin, xf[..., ROT_DIM:]], axis=-1)
    return rot.astype(x.dtype)


def spatial_gating(u, v, ln_g, ln_b, w_s, b_s):
    b, s, _ = v.shape
    nc = s // SG_CHUNK
    vn = layer_norm(v, ln_g, ln_b).reshape(b, nc, SG_CHUNK, SG_GROUPS, SG_GROUP_DIM)
    vz = jnp.einsum('gts,bcsgd->bctgd', w_s, vn) + b_s.T[None, None, :, :, None]
    return u * vz.reshape(b, s, SG_WIDTH)


def banded_blocks(t):
    b, s, kvh, d = t.shape
    nb = s // BLOCK
    tp = jnp.pad(t, ((0, 0), (BLOCK, BLOCK), (0, 0), (0, 0))).reshape(b, nb + 2, BLOCK, kvh, d)
    return jnp.concatenate([tp[:, :-2], tp[:, 1:-1], tp[:, 2:]], axis=2)


def windowed_gqa_sink(q, k, v, sink):
    b, s, _, _ = q.shape
    nb = s // BLOCK
    qb = q.reshape(b, nb, BLOCK, N_KV_HEADS, Q_PER_KV, HEAD_DIM)
    kb = banded_blocks(k)
    vb = banded_blocks(v)
    scores = jnp.einsum('bnqkgd,bnskd->bkgnqs', qb, kb).astype(jnp.float32) * (HEAD_DIM ** -0.5)
    blk = jnp.arange(nb)[:, None, None]
    q_pos = blk * BLOCK + jnp.arange(BLOCK)[None, :, None]
    k_pos = blk * BLOCK - BLOCK + jnp.arange(3 * BLOCK)[None, None, :]
    valid = (jnp.abs(k_pos - q_pos) <= WINDOW) & (k_pos >= 0) & (k_pos < s)
    scores = jnp.where(valid, scores, MASK_VALUE)
    sink_l = jnp.broadcast_to(
        sink.astype(jnp.float32).reshape(N_KV_HEADS, Q_PER_KV)[None, :, :, None, None, None],
        scores.shape[:-1] + (1,))
    probs = jax.nn.softmax(jnp.concatenate([scores, sink_l], axis=-1), axis=-1)[..., :-1]
    out = jnp.einsum('bkgnqs,bnskd->bnqkgd', probs.astype(v.dtype), vb)
    return out.reshape(b, s, Q_WIDTH)


def conv_glu_ffn(h, w_up, conv_w, conv_b, w_down):
    s = h.shape[1]
    up = h @ w_up
    pad = CONV_WIDTH // 2
    up_p = jnp.pad(up, ((0, 0), (pad, pad), (0, 0)))
    conv = conv_b + sum(conv_w[j] * up_p[:, j:j + s] for j in range(CONV_WIDTH))
    gate, val = jnp.split(conv, 2, axis=-1)
    return (jax.nn.gelu(gate, approximate=True) * val) @ w_down


def setup_inputs(seed: int = 0) -> dict:
    key = jax.random.key(seed)
    ks = jax.random.split(key, 20)
    f32 = jnp.float32

    def nrm(k, shape, scale):
        return jax.random.normal(k, shape, f32) * scale

    def gain(k, n):
        return 1.0 + 0.05 * jax.random.normal(k, (DEPTH, n), f32)

    return {
        "x": nrm(ks[0], (BATCH, SEQ, D_MODEL), 1.0),
        "norm_mix_pre": gain(ks[1], D_MODEL),
        "w_in": nrm(ks[2], (DEPTH, D_MODEL, IN_WIDTH), D_MODEL ** -0.5),
        "sg_ln_g": gain(ks[3], SG_WIDTH),
        "sg_ln_b": nrm(ks[4], (DEPTH, SG_WIDTH), 0.02),
        "sg_w": nrm(ks[5], (DEPTH, SG_GROUPS, SG_CHUNK, SG_CHUNK), SG_CHUNK ** -0.5),
        "sg_b": 1.0 + nrm(ks[6], (DEPTH, SG_GROUPS, SG_CHUNK), 0.05),
        "attn_sink": nrm(ks[7], (DEPTH, N_HEADS), 1.0),
        "w_branch_a": nrm(ks[8], (DEPTH, SG_WIDTH, D_MODEL), SG_WIDTH ** -0.5),
        "w_branch_b": nrm(ks[9], (DEPTH, Q_WIDTH, D_MODEL), Q_WIDTH ** -0.5),
        "w_out": nrm(ks[10], (DEPTH, D_MODEL, D_MODEL), D_MODEL ** -0.5),
        "norm_mix_post": gain(ks[11], D_MODEL),
        "norm_ffn_pre": gain(ks[12], D_MODEL),
        "w_up": nrm(ks[13], (DEPTH, D_MODEL, 2 * D_FF), D_MODEL ** -0.5),
        "conv_w": nrm(ks[14], (DEPTH, CONV_WIDTH, 2 * D_FF), CONV_WIDTH ** -0.5),
        "conv_b": nrm(ks[15], (DEPTH, 2 * D_FF), 0.01),
        "w_down": nrm(ks[16], (DEPTH, D_FF, D_MODEL), D_FF ** -0.5),
        "norm_ffn_post": gain(ks[17], D_MODEL),
    }


def reference(x, norm_mix_pre, w_in, sg_ln_g, sg_ln_b, sg_w, sg_b, attn_sink,
              w_branch_a, w_branch_b, w_out, norm_mix_post, norm_ffn_pre,
              w_up, conv_w, conv_b, w_down, norm_ffn_post):
    b, s, _ = x.shape
    split_at = np.cumsum([SG_WIDTH, SG_WIDTH, Q_WIDTH, KV_WIDTH, KV_WIDTH]).tolist()
    for l in range(DEPTH):
        h = rms_norm(x, norm_mix_pre[l])
        z = h @ w_in[l]
        u, v, q, k, vv, g = jnp.split(z, split_at, axis=-1)
        a_out = spatial_gating(jax.nn.gelu(u, approximate=True), jax.nn.gelu(v, approximate=True),
                               sg_ln_g[l], sg_ln_b[l], sg_w[l], sg_b[l])
        q = partial_rope(q.reshape(b, s, N_HEADS, HEAD_DIM))
        k = partial_rope(k.reshape(b, s, N_KV_HEADS, HEAD_DIM))
        vv = vv.reshape(b, s, N_KV_HEADS, HEAD_DIM)
        b_out = windowed_gqa_sink(q, k, vv, attn_sink[l])
        gates = jax.nn.sigmoid(g.reshape(b, s, N_BRANCHES, D_MODEL))
        merged = gates[:, :, 0] * (a_out @ w_branch_a[l]) + gates[:, :, 1] * (b_out @ w_branch_b[l])
        x = x + rms_norm(merged @ w_out[l], norm_mix_post[l])
        hf = rms_norm(x, norm_ffn_pre[l])
        f = conv_glu_ffn(hf, w_up[l], conv_w[l], conv_b[l], w_down[l])
        x = x + rms_norm(f, norm_ffn_post[l])
    return x
```

```python
import functools

import jax
import jax.numpy as jnp
from jax import lax
from jax.experimental import pallas as pl
from jax.experimental.pallas import tpu as pltpu

F32 = jnp.float32
BF16 = jnp.bfloat16

LANES = 128
BF16_SUBLANES = 16
VMEM_BYTES_V7X = 64 * 1024 * 1024

SG_GROUPS = 8
SG_GROUP_DIM = 128
SG_CHUNK = 128
N_HEADS = 8
N_KV_HEADS = 2
HEAD_DIM = 128
Q_PER_KV = N_HEADS // N_KV_HEADS
WINDOW = 128
BLOCK = 128
ROPE_THETA = 500000.0
ROT_DIM = HEAD_DIM // 4
ROT_HALF = ROT_DIM // 2
CONV_WIDTH = 3
EPS = 1e-6
MASK_VALUE = -1e30

IN_TN = 512


def _vmem_limit(nbytes):
    return int(min(VMEM_BYTES_V7X - (4 << 20), nbytes))


def _rms(x, g):
    return x * lax.rsqrt(jnp.mean(x * x, axis=-1, keepdims=True) + EPS) * g


def _rope_table_kernel(invf_ref, c_ref, sa_ref, sb_ref):
    rows = c_ref.shape[0]
    pos = (lax.broadcasted_iota(jnp.int32, (rows, LANES), 0) + pl.program_id(0) * rows).astype(F32)
    ang = pos * invf_ref[...]
    lane = lax.broadcasted_iota(jnp.int32, (rows, LANES), 1)
    s = jnp.sin(ang)
    c_ref[...] = jnp.cos(ang)
    sa_ref[...] = jnp.where(lane < ROT_HALF, -s, 0.0)
    sb_ref[...] = jnp.where((lane >= ROT_HALF) & (lane < ROT_DIM), s, 0.0)


def _rope_tables(seq):
    inv_freq = ROPE_THETA ** (-jnp.arange(0, ROT_DIM, 2, dtype=F32) / ROT_DIM)
    invf = jnp.concatenate([inv_freq, inv_freq, jnp.zeros((LANES - ROT_DIM,), F32)]).reshape(1, LANES)
    rows = min(seq, 512)
    tab = jax.ShapeDtypeStruct((seq, LANES), F32)
    return pl.pallas_call(
        _rope_table_kernel,
        out_shape=(tab, tab, tab),
        grid=(seq // rows,),
        in_specs=[pl.BlockSpec((1, LANES), lambda i: (0, 0))],
        out_specs=tuple(pl.BlockSpec((rows, LANES), lambda i: (i, 0)) for _ in range(3)),
        name="rope_table",
    )(invf)


def _rope_head(z, c, sa, sb):
    return z * c + pltpu.roll(z, HEAD_DIM - ROT_HALF, 1) * sa + pltpu.roll(z, ROT_HALF, 1) * sb


def _in_proj_kernel(x_ref, g_ref, w_ref, c_ref, sa_ref, sb_ref, o_ref, h_ref, *, n_gate, n_uv, n_q):
    j = pl.program_id(1)

    @pl.when(j == 0)
    def _():
        h_ref[...] = _rms(x_ref[...], g_ref[...]).astype(BF16)

    def proj():
        return jnp.dot(h_ref[...], w_ref[...], preferred_element_type=F32)

    @pl.when(j < n_gate)
    def _():
        o_ref[...] = jax.nn.sigmoid(proj()).astype(BF16)

    @pl.when((j >= n_gate) & (j < n_gate + n_uv))
    def _():
        o_ref[...] = jax.nn.gelu(proj(), approximate=True).astype(BF16)

    @pl.when((j >= n_gate + n_uv) & (j < n_gate + n_uv + n_q))
    def _():
        z = proj()
        c, sa, sb = c_ref[...], sa_ref[...], sb_ref[...]
        scale = HEAD_DIM ** -0.5
        for h in range(IN_TN // HEAD_DIM):
            sl = slice(h * HEAD_DIM, (h + 1) * HEAD_DIM)
            o_ref[:, sl] = (_rope_head(z[:, sl], c, sa, sb) * scale).astype(BF16)

    @pl.when(j >= n_gate + n_uv + n_q)
    def _():
        z = proj()
        c, sa, sb = c_ref[...], sa_ref[...], sb_ref[...]
        kw = N_KV_HEADS * HEAD_DIM
        for h in range(N_KV_HEADS):
            sl = slice(h * HEAD_DIM, (h + 1) * HEAD_DIM)
            o_ref[:, sl] = _rope_head(z[:, sl], c, sa, sb).astype(BF16)
        o_ref[:, kw:] = z[:, kw:].astype(BF16)


def _in_proj(x2, gain, w_bf, tabs, *, seq, tm):
    t, d = x2.shape
    n_in = w_bf.shape[1]
    sg_width = SG_GROUPS * SG_GROUP_DIM
    q_width = N_HEADS * HEAD_DIM
    kv_width = N_KV_HEADS * HEAD_DIM
    assert IN_TN == 2 * kv_width
    n_uv = 2 * sg_width // IN_TN
    n_q = q_width // IN_TN
    n_kv = 1
    n_gate = (n_in - 2 * sg_width - q_width - 2 * kv_width) // IN_TN
    nj = n_gate + n_uv + n_q + n_kv
    assert nj * IN_TN == n_in
    first_gate_blk = n_uv + n_q + n_kv
    tiles_per_seq = seq // tm
    kern = functools.partial(_in_proj_kernel, n_gate=n_gate, n_uv=n_uv, n_q=n_q)
    tab_spec = pl.BlockSpec((tm, LANES), lambda i, j: (i % tiles_per_seq, 0))
    vmem = 2 * tm * d * 4 + tm * d * 2 + 2 * d * IN_TN * 2 + 2 * tm * IN_TN * 2 + 6 * tm * LANES * 4
    vmem += 6 * tm * IN_TN * 4
    return pl.pallas_call(
        kern,
        out_shape=jax.ShapeDtypeStruct((t, n_in), BF16),
        grid=(t // tm, nj),
        in_specs=[
            pl.BlockSpec((tm, d), lambda i, j: (i, 0)),
            pl.BlockSpec((1, d), lambda i, j: (0, 0)),
            pl.BlockSpec((d, IN_TN), lambda i, j: (0, (j + first_gate_blk) % nj)),
            tab_spec, tab_spec, tab_spec,
        ],
        out_specs=pl.BlockSpec((tm, IN_TN), lambda i, j: (i, j)),
        scratch_shapes=[pltpu.VMEM((tm, d), BF16)],
        compiler_params=pltpu.CompilerParams(
            dimension_semantics=("parallel", "arbitrary"), vmem_limit_bytes=_vmem_limit(vmem)),
        name="in_proj",
    )(x2, gain, w_bf, *tabs)


def _sgate_kernel(u_ref, v_ref, lng_ref, lnb_ref, w_ref, bt_ref, o_ref):
    tm = u_ref.shape[0]
    v = v_ref[...].astype(F32)
    mu = jnp.mean(v, axis=-1, keepdims=True)
    xc = v - mu
    vn = (xc * lax.rsqrt(jnp.mean(xc * xc, axis=-1, keepdims=True) + EPS) * lng_ref[...]
          + lnb_ref[...]).astype(BF16)
    for c in range(tm // SG_CHUNK):
        rows = slice(c * SG_CHUNK, (c + 1) * SG_CHUNK)
        for g in range(SG_GROUPS):
            cols = slice(g * SG_GROUP_DIM, (g + 1) * SG_GROUP_DIM)
            vz = jnp.dot(w_ref[g], vn[rows, cols], preferred_element_type=F32) + bt_ref[:, g:g + 1]
            o_ref[rows, cols] = (u_ref[rows, cols].astype(F32) * vz).astype(BF16)


def _sgate(z, ln_g, ln_b, w_bf, b_t, *, u_blk, tm):
    t = z.shape[0]
    sg_width = SG_GROUPS * SG_GROUP_DIM
    vmem = 4 * tm * sg_width * 2 + 2 * tm * sg_width * 2 + 4 * tm * sg_width * 4 + (1 << 20)
    return pl.pallas_call(
        _sgate_kernel,
        out_shape=jax.ShapeDtypeStruct((t, sg_width), BF16),
        grid=(t // tm,),
        in_specs=[
            pl.BlockSpec((tm, sg_width), lambda i: (i, u_blk)),
            pl.BlockSpec((tm, sg_width), lambda i: (i, u_blk + 1)),
            pl.BlockSpec((1, sg_width), lambda i: (0, 0)),
            pl.BlockSpec((1, sg_width), lambda i: (0, 0)),
            pl.BlockSpec((SG_GROUPS, SG_CHUNK, SG_CHUNK), lambda i: (0, 0, 0)),
            pl.BlockSpec((SG_CHUNK, SG_GROUPS), lambda i: (0, 0)),
        ],
        out_specs=pl.BlockSpec((tm, sg_width), lambda i: (i, 0)),
        compiler_params=pltpu.CompilerParams(
            dimension_semantics=("parallel",), vmem_limit_bytes=_vmem_limit(vmem)),
        name="sgate",
    )(z, z, ln_g, ln_b, w_bf, b_t)


def _attn_kernel(sink_ref, q_ref, kvp_ref, kvm_ref, kvn_ref, o_ref, kv_ref, *, seq):
    tq = q_ref.shape[0]
    kw = N_KV_HEADS * HEAD_DIM
    kv_ref[0:BLOCK, :] = kvp_ref[...]
    kv_ref[BLOCK:BLOCK + tq, :] = kvm_ref[...]
    kv_ref[BLOCK + tq:, :] = kvn_ref[...]
    blk0 = (pl.program_id(0) * tq) % seq // BLOCK
    qi = lax.broadcasted_iota(jnp.int32, (BLOCK, 3 * BLOCK), 0)
    kj = lax.broadcasted_iota(jnp.int32, (BLOCK, 3 * BLOCK), 1)
    rel = kj - BLOCK - qi
    band = (rel >= -WINDOW) & (rel <= WINDOW)
    for qb in range(tq // BLOCK):
        k_pos = (blk0 + qb - 1) * BLOCK + kj
        valid1 = band & (k_pos >= 0) & (k_pos < seq)
        valid = jnp.concatenate([valid1] * Q_PER_KV, axis=0)
        rows = slice(qb * BLOCK, (qb + 1) * BLOCK)
        win = slice(qb * BLOCK, (qb + 3) * BLOCK)
        for g in range(N_KV_HEADS):
            heads = [g * Q_PER_KV + i for i in range(Q_PER_KV)]
            qg = jnp.concatenate([q_ref[rows, h * HEAD_DIM:(h + 1) * HEAD_DIM] for h in heads], axis=0)
            kwin = kv_ref[win, g * HEAD_DIM:(g + 1) * HEAD_DIM]
            vwin = kv_ref[win, kw + g * HEAD_DIM:kw + (g + 1) * HEAD_DIM]
            s = lax.dot_general(qg, kwin, (((1,), (1,)), ((), ())), preferred_element_type=F32)
            s = jnp.where(valid, s, MASK_VALUE)
            sink = jnp.concatenate(
                [jnp.full((BLOCK, 1), sink_ref[h], F32) for h in heads], axis=0)
            m = jnp.maximum(jnp.max(s, axis=-1, keepdims=True), sink)
            p = jnp.exp(s - m)
            denom = jnp.sum(p, axis=-1, keepdims=True) + jnp.exp(sink - m)
            o = jnp.dot(p.astype(BF16), vwin, preferred_element_type=F32) / denom
            for i, h in enumerate(heads):
                o_ref[rows, h * HEAD_DIM:(h + 1) * HEAD_DIM] = o[i * BLOCK:(i + 1) * BLOCK].astype(BF16)


def _attn(z, sink, *, seq, q_col, kv_col, tq):
    t = z.shape[0]
    q_width = N_HEADS * HEAD_DIM
    kvw = 2 * N_KV_HEADS * HEAD_DIM
    assert q_col % q_width == 0 and kv_col % kvw == 0
    qb_, kb_ = q_col // q_width, kv_col // kvw
    r = tq // BLOCK
    nblk = t // BLOCK
    vmem = 4 * tq * q_width * 2 + 2 * (tq + 2 * BLOCK) * kvw * 2 + (tq + 2 * BLOCK) * kvw * 2
    vmem += 16 * Q_PER_KV * BLOCK * 3 * BLOCK * 4
    return pl.pallas_call(
        functools.partial(_attn_kernel, seq=seq),
        out_shape=jax.ShapeDtypeStruct((t, q_width), BF16),
        grid=(t // tq,),
        in_specs=[
            pl.BlockSpec(memory_space=pltpu.SMEM),
            pl.BlockSpec((tq, q_width), lambda i: (i, qb_)),
            pl.BlockSpec((BLOCK, kvw), lambda i: (jnp.maximum(i * r - 1, 0), kb_)),
            pl.BlockSpec((tq, kvw), lambda i: (i, kb_)),
            pl.BlockSpec((BLOCK, kvw), lambda i: (jnp.minimum((i + 1) * r, nblk - 1), kb_)),
        ],
        out_specs=pl.BlockSpec((tq, q_width), lambda i: (i, 0)),
        scratch_shapes=[pltpu.VMEM((tq + 2 * BLOCK, kvw), BF16)],
        compiler_params=pltpu.CompilerParams(
            dimension_semantics=("parallel",), vmem_limit_bytes=_vmem_limit(vmem)),
        name="attn",
    )(sink, z, z, z, z)


def _merge_kernel(a_ref, b_ref, ga_ref, gb_ref, x_ref, wa_ref, wb_ref, wo_ref, gpost_ref, gpre_ref,
                  x1_ref, hf_ref):
    pa = jnp.dot(a_ref[...], wa_ref[...], preferred_element_type=F32)
    pb = jnp.dot(b_ref[...], wb_ref[...], preferred_element_type=F32)
    merged = ga_ref[...].astype(F32) * pa + gb_ref[...].astype(F32) * pb
    y = jnp.dot(merged.astype(BF16), wo_ref[...], preferred_element_type=F32)
    x1 = x_ref[...] + _rms(y, gpost_ref[...])
    x1_ref[...] = x1
    hf_ref[...] = _rms(x1, gpre_ref[...]).astype(BF16)


def _merge(a, b, z, x2, wa, wb, wo, gpost, gpre, *, tm):
    t, d = x2.shape
    ka, kb = a.shape[1], b.shape[1]
    resident = dict(pipeline_mode=pl.Buffered(1))
    vmem = 2 * tm * (ka + kb) * 2 + 4 * tm * d * 2 + 2 * tm * d * 4 + (ka + kb + d) * d * 2
    vmem += 2 * tm * d * 4 + 2 * tm * d * 2 + 6 * tm * d * 4
    return pl.pallas_call(
        _merge_kernel,
        out_shape=(jax.ShapeDtypeStruct((t, d), F32), jax.ShapeDtypeStruct((t, d), BF16)),
        grid=(t // tm,),
        in_specs=[
            pl.BlockSpec((tm, ka), lambda i: (i, 0)),
            pl.BlockSpec((tm, kb), lambda i: (i, 0)),
            pl.BlockSpec((tm, d), lambda i: (i, 0)),
            pl.BlockSpec((tm, d), lambda i: (i, 1)),
            pl.BlockSpec((tm, d), lambda i: (i, 0)),
            pl.BlockSpec((ka, d), lambda i: (0, 0), **resident),
            pl.BlockSpec((kb, d), lambda i: (0, 0), **resident),
            pl.BlockSpec((d, d), lambda i: (0, 0), **resident),
            pl.BlockSpec((1, d), lambda i: (0, 0)),
            pl.BlockSpec((1, d), lambda i: (0, 0)),
        ],
        out_specs=(pl.BlockSpec((tm, d), lambda i: (i, 0)), pl.BlockSpec((tm, d), lambda i: (i, 0))),
        compiler_params=pltpu.CompilerParams(
            dimension_semantics=("parallel",), vmem_limit_bytes=_vmem_limit(vmem)),
        name="merge",
    )(a, b, z, z, x2, wa, wb, wo, gpost, gpre)


def _ffn_kernel(hp_ref, hm_ref, hn_ref, wg_ref, wv_ref, cwg_ref, cwv_ref, cbg_ref, cbv_ref, wd_ref,
                x1_ref, gpost_ref, o_ref, hx_ref, acc_ref, *, seq):
    i, j = pl.program_id(0), pl.program_id(1)
    tm = hm_ref.shape[0]
    halo = hp_ref.shape[0]

    @pl.when(j == 0)
    def _():
        first = (i * tm) % seq == 0
        last = ((i + 1) * tm) % seq == 0
        hx_ref[0:halo, :] = jnp.where(first, jnp.zeros_like(hp_ref), hp_ref[...])
        hx_ref[halo:halo + tm, :] = hm_ref[...]
        hx_ref[halo + tm:, :] = jnp.where(last, jnp.zeros_like(hn_ref), hn_ref[...])

    def conv(w_ref, cw_ref, cb_ref):
        up = jnp.dot(hx_ref[...], w_ref[...], preferred_element_type=F32)
        cw = cw_ref[...]
        return (cb_ref[...] + cw[0:1] * up[halo - 1:halo - 1 + tm] + cw[1:2] * up[halo:halo + tm]
                + cw[2:3] * up[halo + 1:halo + 1 + tm])

    act = (jax.nn.gelu(conv(wg_ref, cwg_ref, cbg_ref), approximate=True)
           * conv(wv_ref, cwv_ref, cbv_ref)).astype(BF16)
    part = jnp.dot(act, wd_ref[...], preferred_element_type=F32)

    @pl.when(j == 0)
    def _():
        acc_ref[...] = part

    @pl.when(j > 0)
    def _():
        acc_ref[...] += part

    @pl.when(j == pl.num_programs(1) - 1)
    def _():
        o_ref[...] = x1_ref[...] + _rms(acc_ref[...], gpost_ref[...])


def _ffn(hf, x1, w_up, conv_w, conv_b, w_down, gpost, *, seq, tm, tn):
    t, d = x1.shape
    dff = w_down.shape[0]
    assert dff % tn == 0 and tn % LANES == 0
    nj = dff // tn
    halo = BF16_SUBLANES
    r = tm // halo
    nhb = t // halo
    vmem = 2 * (tm + 2 * halo) * d * 2 + (tm + 2 * halo) * d * 2 + 2 * 2 * d * tn * 2 + 2 * tn * d * 2
    vmem += 2 * tm * d * 4 + 2 * tm * d * 4 + tm * d * 4 + 10 * (tm + 2 * halo) * tn * 4
    return pl.pallas_call(
        functools.partial(_ffn_kernel, seq=seq),
        out_shape=jax.ShapeDtypeStruct((t, d), F32),
        grid=(t // tm, nj),
        in_specs=[
            pl.BlockSpec((halo, d), lambda i, j: (jnp.maximum(i * r - 1, 0), 0)),
            pl.BlockSpec((tm, d), lambda i, j: (i, 0)),
            pl.BlockSpec((halo, d), lambda i, j: (jnp.minimum((i + 1) * r, nhb - 1), 0)),
            pl.BlockSpec((d, tn), lambda i, j: (0, j)),
            pl.BlockSpec((d, tn), lambda i, j: (0, j + nj)),
            pl.BlockSpec((CONV_WIDTH, tn), lambda i, j: (0, j)),
            pl.BlockSpec((CONV_WIDTH, tn), lambda i, j: (0, j + nj)),
            pl.BlockSpec((1, tn), lambda i, j: (0, j)),
            pl.BlockSpec((1, tn), lambda i, j: (0, j + nj)),
            pl.BlockSpec((tn, d), lambda i, j: (j, 0)),
            pl.BlockSpec((tm, d), lambda i, j: (i, 0)),
            pl.BlockSpec((1, d), lambda i, j: (0, 0)),
        ],
        out_specs=pl.BlockSpec((tm, d), lambda i, j: (i, 0)),
        scratch_shapes=[pltpu.VMEM((tm + 2 * halo, d), BF16), pltpu.VMEM((tm, d), F32)],
        compiler_params=pltpu.CompilerParams(
            dimension_semantics=("parallel", "arbitrary"), vmem_limit_bytes=_vmem_limit(vmem)),
        name="ffn",
    )(hf, hf, hf, w_up, w_up, conv_w, conv_w, conv_b, conv_b, w_down, x1, gpost)


def kernel(x, norm_mix_pre, w_in, sg_ln_g, sg_ln_b, sg_w, sg_b, attn_sink, w_branch_a, w_branch_b, w_out,
           norm_mix_post, norm_ffn_pre, w_up, conv_w, conv_b, w_down, norm_ffn_post):
    b, s, d = x.shape
    t = b * s
    depth = w_in.shape[0]
    sg_width = SG_GROUPS * SG_GROUP_DIM
    n_in = w_in.shape[-1]
    gate_width = n_in - 2 * sg_width - N_HEADS * HEAD_DIM - 2 * N_KV_HEADS * HEAD_DIM
    assert gate_width == 2 * d
    u_col = gate_width
    q_col = u_col + 2 * sg_width
    kv_col = q_col + N_HEADS * HEAD_DIM
    tabs = _rope_tables(s)
    row = lambda p: p.reshape(1, -1)
    x2 = x.reshape(t, d)
    for l in range(depth):
        z = _in_proj(x2, row(norm_mix_pre[l]), w_in[l].astype(BF16), tabs, seq=s, tm=1024)
        a_out = _sgate(z, row(sg_ln_g[l]), row(sg_ln_b[l]), sg_w[l].astype(BF16), sg_b[l].T,
                       u_blk=u_col // sg_width, tm=512)
        b_out = _attn(z, attn_sink[l], seq=s, q_col=q_col, kv_col=kv_col, tq=512)
        x1, hf = _merge(a_out, b_out, z, x2, w_branch_a[l].astype(BF16), w_branch_b[l].astype(BF16),
                        w_out[l].astype(BF16), row(norm_mix_post[l]), row(norm_ffn_pre[l]), tm=512)
        x2 = _ffn(hf, x1, w_up[l].astype(BF16), conv_w[l], row(conv_b[l]), w_down[l].astype(BF16),
                  row(norm_ffn_post[l]), seq=s, tm=512, tn=512)
    return x2.reshape(b, s, d)
```

```python
import functools

import jax
import jax.numpy as jnp
from jax import lax
from jax.experimental import pallas as pl
from jax.experimental.pallas import tpu as pltpu

F32 = jnp.float32
BF16 = jnp.bfloat16

LANES = 128
F32_SUBLANES = 8
BF16_SUBLANES = 16
VMEM_BYTES_V7X = 64 * 1024 * 1024

SG_GROUPS = 8
SG_GROUP_DIM = 128
SG_CHUNK = 128
N_HEADS = 8
N_KV_HEADS = 2
HEAD_DIM = 128
Q_PER_KV = N_HEADS // N_KV_HEADS
WINDOW = 128
BLOCK = 128
ROPE_THETA = 500000.0
ROT_DIM = HEAD_DIM // 4
ROT_HALF = ROT_DIM // 2
CONV_WIDTH = 3
EPS = 1e-6
MASK_VALUE = -1e30

IN_TN = 512
FFN_SUB = 256


def _vmem_limit(nbytes):
    return int(min(VMEM_BYTES_V7X - (4 << 20), nbytes))


def _rms(x, g):
    return x * lax.rsqrt(jnp.mean(x * x, axis=-1, keepdims=True) + EPS) * g


def _rope_table_kernel(invf_ref, c_ref, sa_ref, sb_ref):
    rows = c_ref.shape[0]
    pos = (lax.broadcasted_iota(jnp.int32, (rows, LANES), 0) + pl.program_id(0) * rows).astype(F32)
    ang = pos * invf_ref[...]
    lane = lax.broadcasted_iota(jnp.int32, (rows, LANES), 1)
    s = jnp.sin(ang)
    c_ref[...] = jnp.cos(ang)
    sa_ref[...] = jnp.where(lane < ROT_HALF, -s, 0.0)
    sb_ref[...] = jnp.where((lane >= ROT_HALF) & (lane < ROT_DIM), s, 0.0)


def _rope_tables(seq):
    inv_freq = ROPE_THETA ** (-jnp.arange(0, ROT_DIM, 2, dtype=F32) / ROT_DIM)
    invf = jnp.concatenate([inv_freq, inv_freq, jnp.zeros((LANES - ROT_DIM,), F32)]).reshape(1, LANES)
    rows = min(seq, 512)
    tab = jax.ShapeDtypeStruct((seq, LANES), F32)
    return pl.pallas_call(
        _rope_table_kernel,
        out_shape=(tab, tab, tab),
        grid=(seq // rows,),
        in_specs=[pl.BlockSpec((1, LANES), lambda i: (0, 0))],
        out_specs=tuple(pl.BlockSpec((rows, LANES), lambda i: (i, 0)) for _ in range(3)),
        name="rope_table",
    )(invf)


def _rope_head(z, c, sa, sb):
    return z * c + pltpu.roll(z, HEAD_DIM - ROT_HALF, 1) * sa + pltpu.roll(z, ROT_HALF, 1) * sb


def _in_proj_kernel(x_ref, g_ref, w_ref, c_ref, sa_ref, sb_ref, o_ref, h_ref, *, n_gate, n_uv, n_q):
    j = pl.program_id(1)

    @pl.when(j == 0)
    def _():
        h_ref[...] = _rms(x_ref[...], g_ref[...]).astype(BF16)

    def proj():
        return jnp.dot(h_ref[...], w_ref[...], preferred_element_type=F32)

    @pl.when(j < n_gate)
    def _():
        o_ref[...] = jax.nn.sigmoid(proj()).astype(BF16)

    @pl.when((j >= n_gate) & (j < n_gate + n_uv))
    def _():
        o_ref[...] = jax.nn.gelu(proj(), approximate=True).astype(BF16)

    @pl.when((j >= n_gate + n_uv) & (j < n_gate + n_uv + n_q))
    def _():
        z = proj()
        c, sa, sb = c_ref[...], sa_ref[...], sb_ref[...]
        scale = HEAD_DIM ** -0.5
        for h in range(IN_TN // HEAD_DIM):
            sl = slice(h * HEAD_DIM, (h + 1) * HEAD_DIM)
            o_ref[:, sl] = (_rope_head(z[:, sl], c, sa, sb) * scale).astype(BF16)

    @pl.when(j >= n_gate + n_uv + n_q)
    def _():
        z = proj()
        c, sa, sb = c_ref[...], sa_ref[...], sb_ref[...]
        kw = N_KV_HEADS * HEAD_DIM
        for h in range(N_KV_HEADS):
            sl = slice(h * HEAD_DIM, (h + 1) * HEAD_DIM)
            o_ref[:, sl] = _rope_head(z[:, sl], c, sa, sb).astype(BF16)
        o_ref[:, kw:] = z[:, kw:].astype(BF16)


def _in_proj(x2, gain, w_bf, tabs, *, seq, tm):
    t, d = x2.shape
    n_in = w_bf.shape[1]
    sg_width = SG_GROUPS * SG_GROUP_DIM
    q_width = N_HEADS * HEAD_DIM
    kv_width = N_KV_HEADS * HEAD_DIM
    assert IN_TN == 2 * kv_width
    n_uv = 2 * sg_width // IN_TN
    n_q = q_width // IN_TN
    n_kv = 1
    n_gate = (n_in - 2 * sg_width - q_width - 2 * kv_width) // IN_TN
    nj = n_gate + n_uv + n_q + n_kv
    assert nj * IN_TN == n_in
    first_gate_blk = n_uv + n_q + n_kv
    tiles_per_seq = seq // tm
    kern = functools.partial(_in_proj_kernel, n_gate=n_gate, n_uv=n_uv, n_q=n_q)
    tab_spec = pl.BlockSpec((tm, LANES), lambda i, j: (i % tiles_per_seq, 0))
    vmem = 2 * tm * d * 4 + tm * d * 2 + 2 * d * IN_TN * 2 + 2 * tm * IN_TN * 2 + 6 * tm * LANES * 4
    vmem += 6 * tm * IN_TN * 4
    return pl.pallas_call(
        kern,
        out_shape=jax.ShapeDtypeStruct((t, n_in), BF16),
        grid=(t // tm, nj),
        in_specs=[
            pl.BlockSpec((tm, d), lambda i, j: (i, 0)),
            pl.BlockSpec((1, d), lambda i, j: (0, 0)),
            pl.BlockSpec((d, IN_TN), lambda i, j: (0, (j + first_gate_blk) % nj)),
            tab_spec, tab_spec, tab_spec,
        ],
        out_specs=pl.BlockSpec((tm, IN_TN), lambda i, j: (i, j)),
        scratch_shapes=[pltpu.VMEM((tm, d), BF16)],
        compiler_params=pltpu.CompilerParams(
            dimension_semantics=("parallel", "arbitrary"), vmem_limit_bytes=_vmem_limit(vmem)),
        name="in_proj",
    )(x2, gain, w_bf, *tabs)


def _sgate_kernel(u_ref, v_ref, lng_ref, lnb_ref, w_ref, bt_ref, o_ref):
    tm = u_ref.shape[0]
    v = v_ref[...].astype(F32)
    mu = jnp.mean(v, axis=-1, keepdims=True)
    xc = v - mu
    vn = (xc * lax.rsqrt(jnp.mean(xc * xc, axis=-1, keepdims=True) + EPS) * lng_ref[...]
          + lnb_ref[...]).astype(BF16)
    for c in range(tm // SG_CHUNK):
        rows = slice(c * SG_CHUNK, (c + 1) * SG_CHUNK)
        for g in range(SG_GROUPS):
            cols = slice(g * SG_GROUP_DIM, (g + 1) * SG_GROUP_DIM)
            vz = jnp.dot(w_ref[g], vn[rows, cols], preferred_element_type=F32) + bt_ref[:, g:g + 1]
            o_ref[rows, cols] = (u_ref[rows, cols].astype(F32) * vz).astype(BF16)


def _sgate(z, ln_g, ln_b, w_bf, b_t, *, u_blk, tm):
    t = z.shape[0]
    sg_width = SG_GROUPS * SG_GROUP_DIM
    vmem = 4 * tm * sg_width * 2 + 2 * tm * sg_width * 2 + 4 * tm * sg_width * 4 + (1 << 20)
    return pl.pallas_call(
        _sgate_kernel,
        out_shape=jax.ShapeDtypeStruct((t, sg_width), BF16),
        grid=(t // tm,),
        in_specs=[
            pl.BlockSpec((tm, sg_width), lambda i: (i, u_blk)),
            pl.BlockSpec((tm, sg_width), lambda i: (i, u_blk + 1)),
            pl.BlockSpec((1, sg_width), lambda i: (0, 0)),
            pl.BlockSpec((1, sg_width), lambda i: (0, 0)),
            pl.BlockSpec((SG_GROUPS, SG_CHUNK, SG_CHUNK), lambda i: (0, 0, 0)),
            pl.BlockSpec((SG_CHUNK, SG_GROUPS), lambda i: (0, 0)),
        ],
        out_specs=pl.BlockSpec((tm, sg_width), lambda i: (i, 0)),
        compiler_params=pltpu.CompilerParams(
            dimension_semantics=("parallel",), vmem_limit_bytes=_vmem_limit(vmem)),
        name="sgate",
    )(z, z, ln_g, ln_b, w_bf, b_t)


def _attn_kernel(sink_ref, q_ref, kvp_ref, kvm_ref, kvn_ref, o_ref, kv_ref, *, seq):
    tq = q_ref.shape[0]
    kw = N_KV_HEADS * HEAD_DIM
    kv_ref[0:BLOCK, :] = kvp_ref[...]
    kv_ref[BLOCK:BLOCK + tq, :] = kvm_ref[...]
    kv_ref[BLOCK + tq:, :] = kvn_ref[...]
    blk0 = (pl.program_id(0) * tq) % seq // BLOCK
    qi = lax.broadcasted_iota(jnp.int32, (BLOCK, 3 * BLOCK), 0)
    kj = lax.broadcasted_iota(jnp.int32, (BLOCK, 3 * BLOCK), 1)
    rel = kj - BLOCK - qi
    band = (rel >= -WINDOW) & (rel <= WINDOW)
    for qb in range(tq // BLOCK):
        k_pos = (blk0 + qb - 1) * BLOCK + kj
        valid1 = band & (k_pos >= 0) & (k_pos < seq)
        valid = jnp.concatenate([valid1] * Q_PER_KV, axis=0)
        rows = slice(qb * BLOCK, (qb + 1) * BLOCK)
        win = slice(qb * BLOCK, (qb + 3) * BLOCK)
        for g in range(N_KV_HEADS):
            heads = [g * Q_PER_KV + i for i in range(Q_PER_KV)]
            qg = jnp.concatenate([q_ref[rows, h * HEAD_DIM:(h + 1) * HEAD_DIM] for h in heads], axis=0)
            kwin = kv_ref[win, g * HEAD_DIM:(g + 1) * HEAD_DIM]
            vwin = kv_ref[win, kw + g * HEAD_DIM:kw + (g + 1) * HEAD_DIM]
            s = lax.dot_general(qg, kwin, (((1,), (1,)), ((), ())), preferred_element_type=F32)
            s = jnp.where(valid, s, MASK_VALUE)
            sink = jnp.concatenate(
                [jnp.full((BLOCK, 1), sink_ref[h], F32) for h in heads], axis=0)
            m = jnp.maximum(jnp.max(s, axis=-1, keepdims=True), sink)
            p = jnp.exp(s - m)
            denom = jnp.sum(p, axis=-1, keepdims=True) + jnp.exp(sink - m)
            o = jnp.dot(p.astype(BF16), vwin, preferred_element_type=F32) / denom
            for i, h in enumerate(heads):
                o_ref[rows, h * HEAD_DIM:(h + 1) * HEAD_DIM] = o[i * BLOCK:(i + 1) * BLOCK].astype(BF16)


def _attn(z, sink, *, seq, q_col, kv_col, tq):
    t = z.shape[0]
    q_width = N_HEADS * HEAD_DIM
    kvw = 2 * N_KV_HEADS * HEAD_DIM
    assert q_col % q_width == 0 and kv_col % kvw == 0
    qb_, kb_ = q_col // q_width, kv_col // kvw
    r = tq // BLOCK
    nblk = t // BLOCK
    vmem = 4 * tq * q_width * 2 + 2 * (tq + 2 * BLOCK) * kvw * 2 + (tq + 2 * BLOCK) * kvw * 2
    vmem += 16 * Q_PER_KV * BLOCK * 3 * BLOCK * 4
    return pl.pallas_call(
        functools.partial(_attn_kernel, seq=seq),
        out_shape=jax.ShapeDtypeStruct((t, q_width), BF16),
        grid=(t // tq,),
        in_specs=[
            pl.BlockSpec(memory_space=pltpu.SMEM),
            pl.BlockSpec((tq, q_width), lambda i: (i, qb_)),
            pl.BlockSpec((BLOCK, kvw), lambda i: (jnp.maximum(i * r - 1, 0), kb_)),
            pl.BlockSpec((tq, kvw), lambda i: (i, kb_)),
            pl.BlockSpec((BLOCK, kvw), lambda i: (jnp.minimum((i + 1) * r, nblk - 1), kb_)),
        ],
        out_specs=pl.BlockSpec((tq, q_width), lambda i: (i, 0)),
        scratch_shapes=[pltpu.VMEM((tq + 2 * BLOCK, kvw), BF16)],
        compiler_params=pltpu.CompilerParams(
            dimension_semantics=("parallel",), vmem_limit_bytes=_vmem_limit(vmem)),
        name="attn",
    )(sink, z, z, z, z)


def _merge_kernel(a_ref, b_ref, ga_ref, gb_ref, x_ref, wa_ref, wb_ref, wo_ref, gpost_ref, gpre_ref,
                  x1_ref, hf_ref):
    pa = jnp.dot(a_ref[...], wa_ref[...], preferred_element_type=F32)
    pb = jnp.dot(b_ref[...], wb_ref[...], preferred_element_type=F32)
    merged = ga_ref[...].astype(F32) * pa + gb_ref[...].astype(F32) * pb
    y = jnp.dot(merged.astype(BF16), wo_ref[...], preferred_element_type=F32)
    x1 = x_ref[...] + _rms(y, gpost_ref[...])
    x1_ref[...] = x1
    hf_ref[...] = _rms(x1, gpre_ref[...]).astype(BF16)


def _merge(a, b, z, x2, wa, wb, wo, gpost, gpre, *, tm):
    t, d = x2.shape
    ka, kb = a.shape[1], b.shape[1]
    resident = dict(pipeline_mode=pl.Buffered(1))
    vmem = 2 * tm * (ka + kb) * 2 + 4 * tm * d * 2 + 2 * tm * d * 4 + (ka + kb + d) * d * 2
    vmem += 2 * tm * d * 4 + 2 * tm * d * 2 + 6 * tm * d * 4
    return pl.pallas_call(
        _merge_kernel,
        out_shape=(jax.ShapeDtypeStruct((t, d), F32), jax.ShapeDtypeStruct((t, d), BF16)),
        grid=(t // tm,),
        in_specs=[
            pl.BlockSpec((tm, ka), lambda i: (i, 0)),
            pl.BlockSpec((tm, kb), lambda i: (i, 0)),
            pl.BlockSpec((tm, d), lambda i: (i, 0)),
            pl.BlockSpec((tm, d), lambda i: (i, 1)),
            pl.BlockSpec((tm, d), lambda i: (i, 0)),
            pl.BlockSpec((ka, d), lambda i: (0, 0), **resident),
            pl.BlockSpec((kb, d), lambda i: (0, 0), **resident),
            pl.BlockSpec((d, d), lambda i: (0, 0), **resident),
            pl.BlockSpec((1, d), lambda i: (0, 0)),
            pl.BlockSpec((1, d), lambda i: (0, 0)),
        ],
        out_specs=(pl.BlockSpec((tm, d), lambda i: (i, 0)), pl.BlockSpec((tm, d), lambda i: (i, 0))),
        compiler_params=pltpu.CompilerParams(
            dimension_semantics=("parallel",), vmem_limit_bytes=_vmem_limit(vmem)),
        name="merge",
    )(a, b, z, z, x2, wa, wb, wo, gpost, gpre)


def _ffn_kernel(hp_ref, hm_ref, hn_ref, wg_ref, wv_ref, cwg_ref, cwv_ref, cbg_ref, cbv_ref, wd_ref,
                x1_ref, gpost_ref, o_ref, src_ref, hx_ref, acc_ref, ug_ref, uv_ref, actf_ref, *, seq):
    i, j = pl.program_id(0), pl.program_id(1)
    tm = hm_ref.shape[0]
    ext = hx_ref.shape[0]
    pitch = ext // F32_SUBLANES
    hal = F32_SUBLANES
    n_slab_in = hm_ref.shape[1] // LANES

    @pl.when(j == 0)
    def _():
        first = (i * tm) % seq == 0
        last = ((i + 1) * tm) % seq == 0
        hp = hp_ref[hp_ref.shape[0] - hal:, :].astype(F32)
        hn = hn_ref[0:hal, :].astype(F32)
        hp = jnp.where(first, jnp.zeros_like(hp), hp)
        hn = jnp.where(last, jnp.zeros_like(hn), hn)
        for l in range(n_slab_in):
            lanes = slice(l * LANES, (l + 1) * LANES)
            src_ref[l, 0:hal, :] = hp[:, lanes]
            src_ref[l, hal:hal + tm, :] = hm_ref[:, lanes].astype(F32)
            src_ref[l, hal + tm:, :] = hn[:, lanes]

        def interleave(w, carry):
            r0 = pl.multiple_of(w * BF16_SUBLANES, BF16_SUBLANES)
            for l in range(n_slab_in):
                a = src_ref[l, pl.ds(2 * w, F32_SUBLANES, stride=pitch), :]
                b = src_ref[l, pl.ds(2 * w + 1, F32_SUBLANES, stride=pitch), :]
                hx_ref[pl.ds(r0, BF16_SUBLANES), l * LANES:(l + 1) * LANES] = (
                    jnp.concatenate([a, b], axis=0).astype(BF16))
            return carry

        lax.fori_loop(0, ext // BF16_SUBLANES, interleave, 0)
        acc_ref[...] = jnp.zeros_like(acc_ref)

    def conv(u, cw, cb):
        body = ext - F32_SUBLANES
        prev = jnp.concatenate([pltpu.roll(u[body:], 1, 0), u[:body]], axis=0)
        nxt = jnp.concatenate([u[F32_SUBLANES:], pltpu.roll(u[:F32_SUBLANES], F32_SUBLANES - 1, 0)], axis=0)
        return cb + cw[0:1] * prev + cw[1:2] * u + cw[2:3] * nxt

    tn = wg_ref.shape[1]
    subs = [slice(c * FFN_SUB, (c + 1) * FFN_SUB) for c in range(tn // FFN_SUB)]
    for cols in subs:
        ug_ref[:, cols] = jnp.dot(hx_ref[...], wg_ref[:, cols], preferred_element_type=F32)
        uv_ref[:, cols] = jnp.dot(hx_ref[...], wv_ref[:, cols], preferred_element_type=F32)
    for c, cols in enumerate(subs):
        act = (jax.nn.gelu(conv(ug_ref[:, cols], cwg_ref[:, cols], cbg_ref[:, cols]), approximate=True)
               * conv(uv_ref[:, cols], cwv_ref[:, cols], cbv_ref[:, cols]))
        for l in range(FFN_SUB // LANES):
            for v in range(pitch):
                actf_ref[c * (FFN_SUB // LANES) + l, pl.ds(v, F32_SUBLANES, stride=pitch), :] = (
                    act[v * F32_SUBLANES:(v + 1) * F32_SUBLANES, l * LANES:(l + 1) * LANES])
        lhs = jnp.concatenate(
            [actf_ref[c * (FFN_SUB // LANES) + l, hal:hal + tm, :] for l in range(FFN_SUB // LANES)],
            axis=1).astype(BF16)
        acc_ref[...] += jnp.dot(lhs, wd_ref[cols, :], preferred_element_type=F32)

    @pl.when(j == pl.num_programs(1) - 1)
    def _():
        o_ref[...] = x1_ref[...] + _rms(acc_ref[...], gpost_ref[...])


def _ffn(hf, x1, w_up, conv_w, conv_b, w_down, gpost, *, seq, tm, tn):
    t, d = x1.shape
    dff = w_down.shape[0]
    assert dff % tn == 0 and tn % FFN_SUB == 0
    nj = dff // tn
    halo = BF16_SUBLANES
    r = tm // halo
    nhb = t // halo
    ext = tm + 2 * F32_SUBLANES
    assert ext % BF16_SUBLANES == 0 and d % LANES == 0
    vmem = 2 * (tm + 2 * halo) * d * 2 + ext * d * (4 + 2) + 2 * 2 * d * tn * 2 + 2 * tn * d * 2
    vmem += 2 * tm * d * 4 + 2 * tm * d * 4 + tm * d * 4 + 3 * ext * tn * 4 + 8 * tm * tn * 4
    return pl.pallas_call(
        functools.partial(_ffn_kernel, seq=seq),
        out_shape=jax.ShapeDtypeStruct((t, d), F32),
        grid=(t // tm, nj),
        in_specs=[
            pl.BlockSpec((halo, d), lambda i, j: (jnp.maximum(i * r - 1, 0), 0)),
            pl.BlockSpec((tm, d), lambda i, j: (i, 0)),
            pl.BlockSpec((halo, d), lambda i, j: (jnp.minimum((i + 1) * r, nhb - 1), 0)),
            pl.BlockSpec((d, tn), lambda i, j: (0, j)),
            pl.BlockSpec((d, tn), lambda i, j: (0, j + nj)),
            pl.BlockSpec((CONV_WIDTH, tn), lambda i, j: (0, j)),
            pl.BlockSpec((CONV_WIDTH, tn), lambda i, j: (0, j + nj)),
            pl.BlockSpec((1, tn), lambda i, j: (0, j)),
            pl.BlockSpec((1, tn), lambda i, j: (0, j + nj)),
            pl.BlockSpec((tn, d), lambda i, j: (j, 0)),
            pl.BlockSpec((tm, d), lambda i, j: (i, 0)),
            pl.BlockSpec((1, d), lambda i, j: (0, 0)),
        ],
        out_specs=pl.BlockSpec((tm, d), lambda i, j: (i, 0)),
        scratch_shapes=[pltpu.VMEM((d // LANES, ext, LANES), F32), pltpu.VMEM((ext, d), BF16),
                        pltpu.VMEM((tm, d), F32), pltpu.VMEM((ext, tn), F32), pltpu.VMEM((ext, tn), F32),
                        pltpu.VMEM((tn // LANES, ext, LANES), F32)],
        compiler_params=pltpu.CompilerParams(
            dimension_semantics=("parallel", "arbitrary"), vmem_limit_bytes=_vmem_limit(vmem)),
        name="ffn",
    )(hf, hf, hf, w_up, w_up, conv_w, conv_w, conv_b, conv_b, w_down, x1, gpost)


def kernel(x, norm_mix_pre, w_in, sg_ln_g, sg_ln_b, sg_w, sg_b, attn_sink, w_branch_a, w_branch_b, w_out,
           norm_mix_post, norm_ffn_pre, w_up, conv_w, conv_b, w_down, norm_ffn_post):
    b, s, d = x.shape
    t = b * s
    depth = w_in.shape[0]
    sg_width = SG_GROUPS * SG_GROUP_DIM
    n_in = w_in.shape[-1]
    gate_width = n_in - 2 * sg_width - N_HEADS * HEAD_DIM - 2 * N_KV_HEADS * HEAD_DIM
    assert gate_width == 2 * d
    u_col = gate_width
    q_col = u_col + 2 * sg_width
    kv_col = q_col + N_HEADS * HEAD_DIM
    tabs = _rope_tables(s)
    row = lambda p: p.reshape(1, -1)
    x2 = x.reshape(t, d)
    for l in range(depth):
        z = _in_proj(x2, row(norm_mix_pre[l]), w_in[l].astype(BF16), tabs, seq=s, tm=1024)
        a_out = _sgate(z, row(sg_ln_g[l]), row(sg_ln_b[l]), sg_w[l].astype(BF16), sg_b[l].T,
                       u_blk=u_col // sg_width, tm=512)
        b_out = _attn(z, attn_sink[l], seq=s, q_col=q_col, kv_col=kv_col, tq=512)
        x1, hf = _merge(a_out, b_out, z, x2, w_branch_a[l].astype(BF16), w_branch_b[l].astype(BF16),
                        w_out[l].astype(BF16), row(norm_mix_post[l]), row(norm_ffn_pre[l]), tm=512)
        x2 = _ffn(hf, x1, w_up[l].astype(BF16), conv_w[l], row(conv_b[l]), w_down[l].astype(BF16),
                  row(norm_ffn_post[l]), seq=s, tm=512, tn=512)
    return x2.reshape(b, s, d)
```

```python
import functools

import jax
import jax.numpy as jnp
from jax import lax
from jax.experimental import pallas as pl
from jax.experimental.pallas import tpu as pltpu

F32 = jnp.float32
BF16 = jnp.bfloat16

LANES = 128
F32_SUBLANES = 8
BF16_SUBLANES = 16
VMEM_BYTES_V7X = 64 * 1024 * 1024

SG_GROUPS = 8
SG_GROUP_DIM = 128
SG_CHUNK = 128
N_HEADS = 8
N_KV_HEADS = 2
HEAD_DIM = 128
Q_PER_KV = N_HEADS // N_KV_HEADS
WINDOW = 128
BLOCK = 128
ROPE_THETA = 500000.0
ROT_DIM = HEAD_DIM // 4
ROT_HALF = ROT_DIM // 2
CONV_WIDTH = 3
EPS = 1e-6
MASK_VALUE = -1e30

IN_TN = 512
FFN_SUB = 256


def _vmem_limit(nbytes):
    return int(min(VMEM_BYTES_V7X - (4 << 20), nbytes))


def _rms(x, g):
    return x * lax.rsqrt(jnp.mean(x * x, axis=-1, keepdims=True) + EPS) * g


def _rope_table_kernel(invf_ref, c_ref, sa_ref, sb_ref):
    rows = c_ref.shape[0]
    pos = (lax.broadcasted_iota(jnp.int32, (rows, LANES), 0) + pl.program_id(0) * rows).astype(F32)
    ang = pos * invf_ref[...]
    lane = lax.broadcasted_iota(jnp.int32, (rows, LANES), 1)
    s = jnp.sin(ang)
    c_ref[...] = jnp.cos(ang)
    sa_ref[...] = jnp.where(lane < ROT_HALF, -s, 0.0)
    sb_ref[...] = jnp.where((lane >= ROT_HALF) & (lane < ROT_DIM), s, 0.0)


def _rope_tables(seq):
    inv_freq = ROPE_THETA ** (-jnp.arange(0, ROT_DIM, 2, dtype=F32) / ROT_DIM)
    invf = jnp.concatenate([inv_freq, inv_freq, jnp.zeros((LANES - ROT_DIM,), F32)]).reshape(1, LANES)
    rows = min(seq, 512)
    tab = jax.ShapeDtypeStruct((seq, LANES), F32)
    return pl.pallas_call(
        _rope_table_kernel,
        out_shape=(tab, tab, tab),
        grid=(seq // rows,),
        in_specs=[pl.BlockSpec((1, LANES), lambda i: (0, 0))],
        out_specs=tuple(pl.BlockSpec((rows, LANES), lambda i: (i, 0)) for _ in range(3)),
        name="rope_table",
    )(invf)


def _rope_head(z, c, sa, sb):
    return z * c + pltpu.roll(z, HEAD_DIM - ROT_HALF, 1) * sa + pltpu.roll(z, ROT_HALF, 1) * sb


def _in_proj_kernel(x_ref, g_ref, w_ref, c_ref, sa_ref, sb_ref, o_ref, h_ref, *, n_gate, n_uv, n_q):
    j = pl.program_id(1)

    @pl.when(j == 0)
    def _():
        h_ref[...] = _rms(x_ref[...], g_ref[...]).astype(BF16)

    def proj():
        return jnp.dot(h_ref[...], w_ref[...], preferred_element_type=F32)

    @pl.when(j < n_gate)
    def _():
        o_ref[...] = jax.nn.sigmoid(proj()).astype(BF16)

    @pl.when((j >= n_gate) & (j < n_gate + n_uv))
    def _():
        o_ref[...] = jax.nn.gelu(proj(), approximate=True).astype(BF16)

    @pl.when((j >= n_gate + n_uv) & (j < n_gate + n_uv + n_q))
    def _():
        z = proj()
        c, sa, sb = c_ref[...], sa_ref[...], sb_ref[...]
        scale = HEAD_DIM ** -0.5
        for h in range(IN_TN // HEAD_DIM):
            sl = slice(h * HEAD_DIM, (h + 1) * HEAD_DIM)
            o_ref[:, sl] = (_rope_head(z[:, sl], c, sa, sb) * scale).astype(BF16)

    @pl.when(j >= n_gate + n_uv + n_q)
    def _():
        z = proj()
        c, sa, sb = c_ref[...], sa_ref[...], sb_ref[...]
        kw = N_KV_HEADS * HEAD_DIM
        for h in range(N_KV_HEADS):
            sl = slice(h * HEAD_DIM, (h + 1) * HEAD_DIM)
            o_ref[:, sl] = _rope_head(z[:, sl], c, sa, sb).astype(BF16)
        o_ref[:, kw:] = z[:, kw:].astype(BF16)


def _in_proj(x2, gain, w_bf, tabs, *, seq, tm):
    t, d = x2.shape
    n_in = w_bf.shape[1]
    sg_width = SG_GROUPS * SG_GROUP_DIM
    q_width = N_HEADS * HEAD_DIM
    kv_width = N_KV_HEADS * HEAD_DIM
    assert IN_TN == 2 * kv_width
    n_uv = 2 * sg_width // IN_TN
    n_q = q_width // IN_TN
    n_kv = 1
    n_gate = (n_in - 2 * sg_width - q_width - 2 * kv_width) // IN_TN
    nj = n_gate + n_uv + n_q + n_kv
    assert nj * IN_TN == n_in
    first_gate_blk = n_uv + n_q + n_kv
    tiles_per_seq = seq // tm
    kern = functools.partial(_in_proj_kernel, n_gate=n_gate, n_uv=n_uv, n_q=n_q)
    tab_spec = pl.BlockSpec((tm, LANES), lambda i, j: (i % tiles_per_seq, 0))
    vmem = 2 * tm * d * 4 + tm * d * 2 + 2 * d * IN_TN * 2 + 2 * tm * IN_TN * 2 + 6 * tm * LANES * 4
    vmem += 6 * tm * IN_TN * 4
    return pl.pallas_call(
        kern,
        out_shape=jax.ShapeDtypeStruct((t, n_in), BF16),
        grid=(t // tm, nj),
        in_specs=[
            pl.BlockSpec((tm, d), lambda i, j: (i, 0)),
            pl.BlockSpec((1, d), lambda i, j: (0, 0)),
            pl.BlockSpec((None, d, IN_TN), lambda i, j: ((j + first_gate_blk) % nj, 0, 0)),
            tab_spec, tab_spec, tab_spec,
        ],
        out_specs=pl.BlockSpec((tm, IN_TN), lambda i, j: (i, j)),
        scratch_shapes=[pltpu.VMEM((tm, d), BF16)],
        compiler_params=pltpu.CompilerParams(
            dimension_semantics=("parallel", "arbitrary"), vmem_limit_bytes=_vmem_limit(vmem)),
        name="in_proj",
    )(x2, gain, w_bf.reshape(d, nj, IN_TN).transpose(1, 0, 2), *tabs)


def _sgate_kernel(u_ref, v_ref, lng_ref, lnb_ref, w_ref, bt_ref, o_ref):
    tm = u_ref.shape[0]
    v = v_ref[...].astype(F32)
    mu = jnp.mean(v, axis=-1, keepdims=True)
    xc = v - mu
    vn = (xc * lax.rsqrt(jnp.mean(xc * xc, axis=-1, keepdims=True) + EPS) * lng_ref[...]
          + lnb_ref[...]).astype(BF16)
    for c in range(tm // SG_CHUNK):
        rows = slice(c * SG_CHUNK, (c + 1) * SG_CHUNK)
        for g in range(SG_GROUPS):
            cols = slice(g * SG_GROUP_DIM, (g + 1) * SG_GROUP_DIM)
            vz = jnp.dot(w_ref[g], vn[rows, cols], preferred_element_type=F32) + bt_ref[:, g:g + 1]
            o_ref[rows, cols] = (u_ref[rows, cols].astype(F32) * vz).astype(BF16)


def _sgate(z, ln_g, ln_b, w_bf, b_t, *, u_blk, tm):
    t = z.shape[0]
    sg_width = SG_GROUPS * SG_GROUP_DIM
    vmem = 4 * tm * sg_width * 2 + 2 * tm * sg_width * 2 + 4 * tm * sg_width * 4 + (1 << 20)
    return pl.pallas_call(
        _sgate_kernel,
        out_shape=jax.ShapeDtypeStruct((t, sg_width), BF16),
        grid=(t // tm,),
        in_specs=[
            pl.BlockSpec((tm, sg_width), lambda i: (i, u_blk)),
            pl.BlockSpec((tm, sg_width), lambda i: (i, u_blk + 1)),
            pl.BlockSpec((1, sg_width), lambda i: (0, 0)),
            pl.BlockSpec((1, sg_width), lambda i: (0, 0)),
            pl.BlockSpec((SG_GROUPS, SG_CHUNK, SG_CHUNK), lambda i: (0, 0, 0)),
            pl.BlockSpec((SG_CHUNK, SG_GROUPS), lambda i: (0, 0)),
        ],
        out_specs=pl.BlockSpec((tm, sg_width), lambda i: (i, 0)),
        compiler_params=pltpu.CompilerParams(
            dimension_semantics=("parallel",), vmem_limit_bytes=_vmem_limit(vmem)),
        name="sgate",
    )(z, z, ln_g, ln_b, w_bf, b_t)


def _attn_kernel(sink_ref, q_ref, kvp_ref, kvm_ref, kvn_ref, o_ref, kv_ref, *, seq):
    tq = q_ref.shape[0]
    kw = N_KV_HEADS * HEAD_DIM
    kv_ref[0:BLOCK, :] = kvp_ref[...]
    kv_ref[BLOCK:BLOCK + tq, :] = kvm_ref[...]
    kv_ref[BLOCK + tq:, :] = kvn_ref[...]
    blk0 = (pl.program_id(0) * tq) % seq // BLOCK
    qi = lax.broadcasted_iota(jnp.int32, (BLOCK, 3 * BLOCK), 0)
    kj = lax.broadcasted_iota(jnp.int32, (BLOCK, 3 * BLOCK), 1)
    rel = kj - BLOCK - qi
    band = (rel >= -WINDOW) & (rel <= WINDOW)
    for qb in range(tq // BLOCK):
        k_pos = (blk0 + qb - 1) * BLOCK + kj
        valid1 = band & (k_pos >= 0) & (k_pos < seq)
        valid = jnp.concatenate([valid1] * Q_PER_KV, axis=0)
        rows = slice(qb * BLOCK, (qb + 1) * BLOCK)
        win = slice(qb * BLOCK, (qb + 3) * BLOCK)
        for g in range(N_KV_HEADS):
            heads = [g * Q_PER_KV + i for i in range(Q_PER_KV)]
            qg = jnp.concatenate([q_ref[rows, h * HEAD_DIM:(h + 1) * HEAD_DIM] for h in heads], axis=0)
            kwin = kv_ref[win, g * HEAD_DIM:(g + 1) * HEAD_DIM]
            vwin = kv_ref[win, kw + g * HEAD_DIM:kw + (g + 1) * HEAD_DIM]
            s = lax.dot_general(qg, kwin, (((1,), (1,)), ((), ())), preferred_element_type=F32)
            s = jnp.where(valid, s, MASK_VALUE)
            sink = jnp.concatenate(
                [jnp.full((BLOCK, 1), sink_ref[h], F32) for h in heads], axis=0)
            m = jnp.maximum(jnp.max(s, axis=-1, keepdims=True), sink)
            p = jnp.exp(s - m)
            denom = jnp.sum(p, axis=-1, keepdims=True) + jnp.exp(sink - m)
            o = jnp.dot(p.astype(BF16), vwin, preferred_element_type=F32) / denom
            for i, h in enumerate(heads):
                o_ref[rows, h * HEAD_DIM:(h + 1) * HEAD_DIM] = o[i * BLOCK:(i + 1) * BLOCK].astype(BF16)


def _attn(z, sink, *, seq, q_col, kv_col, tq):
    t = z.shape[0]
    q_width = N_HEADS * HEAD_DIM
    kvw = 2 * N_KV_HEADS * HEAD_DIM
    assert q_col % q_width == 0 and kv_col % kvw == 0
    qb_, kb_ = q_col // q_width, kv_col // kvw
    r = tq // BLOCK
    nblk = t // BLOCK
    vmem = 4 * tq * q_width * 2 + 2 * (tq + 2 * BLOCK) * kvw * 2 + (tq + 2 * BLOCK) * kvw * 2
    vmem += 16 * Q_PER_KV * BLOCK * 3 * BLOCK * 4
    return pl.pallas_call(
        functools.partial(_attn_kernel, seq=seq),
        out_shape=jax.ShapeDtypeStruct((t, q_width), BF16),
        grid=(t // tq,),
        in_specs=[
            pl.BlockSpec(memory_space=pltpu.SMEM),
            pl.BlockSpec((tq, q_width), lambda i: (i, qb_)),
            pl.BlockSpec((BLOCK, kvw), lambda i: (jnp.maximum(i * r - 1, 0), kb_)),
            pl.BlockSpec((tq, kvw), lambda i: (i, kb_)),
            pl.BlockSpec((BLOCK, kvw), lambda i: (jnp.minimum((i + 1) * r, nblk - 1), kb_)),
        ],
        out_specs=pl.BlockSpec((tq, q_width), lambda i: (i, 0)),
        scratch_shapes=[pltpu.VMEM((tq + 2 * BLOCK, kvw), BF16)],
        compiler_params=pltpu.CompilerParams(
            dimension_semantics=("parallel",), vmem_limit_bytes=_vmem_limit(vmem)),
        name="attn",
    )(sink, z, z, z, z)


def _merge_kernel(a_ref, b_ref, ga_ref, gb_ref, x_ref, wa_ref, wb_ref, wo_ref, gpost_ref, gpre_ref,
                  x1_ref, hf_ref):
    pa = jnp.dot(a_ref[...], wa_ref[...], preferred_element_type=F32)
    pb = jnp.dot(b_ref[...], wb_ref[...], preferred_element_type=F32)
    merged = ga_ref[...].astype(F32) * pa + gb_ref[...].astype(F32) * pb
    y = jnp.dot(merged.astype(BF16), wo_ref[...], preferred_element_type=F32)
    x1 = x_ref[...] + _rms(y, gpost_ref[...])
    x1_ref[...] = x1
    hf_ref[...] = _rms(x1, gpre_ref[...]).astype(BF16)


def _merge(a, b, z, x2, wa, wb, wo, gpost, gpre, *, tm):
    t, d = x2.shape
    ka, kb = a.shape[1], b.shape[1]
    resident = dict(pipeline_mode=pl.Buffered(1))
    vmem = 2 * tm * (ka + kb) * 2 + 4 * tm * d * 2 + 2 * tm * d * 4 + (ka + kb + d) * d * 2
    vmem += 2 * tm * d * 4 + 2 * tm * d * 2 + 6 * tm * d * 4
    return pl.pallas_call(
        _merge_kernel,
        out_shape=(jax.ShapeDtypeStruct((t, d), F32), jax.ShapeDtypeStruct((t, d), BF16)),
        grid=(t // tm,),
        in_specs=[
            pl.BlockSpec((tm, ka), lambda i: (i, 0)),
            pl.BlockSpec((tm, kb), lambda i: (i, 0)),
            pl.BlockSpec((tm, d), lambda i: (i, 0)),
            pl.BlockSpec((tm, d), lambda i: (i, 1)),
            pl.BlockSpec((tm, d), lambda i: (i, 0)),
            pl.BlockSpec((ka, d), lambda i: (0, 0), **resident),
            pl.BlockSpec((kb, d), lambda i: (0, 0), **resident),
            pl.BlockSpec((d, d), lambda i: (0, 0), **resident),
            pl.BlockSpec((1, d), lambda i: (0, 0)),
            pl.BlockSpec((1, d), lambda i: (0, 0)),
        ],
        out_specs=(pl.BlockSpec((tm, d), lambda i: (i, 0)), pl.BlockSpec((tm, d), lambda i: (i, 0))),
        compiler_params=pltpu.CompilerParams(
            dimension_semantics=("parallel",), vmem_limit_bytes=_vmem_limit(vmem)),
        name="merge",
    )(a, b, z, z, x2, wa, wb, wo, gpost, gpre)


def _ffn_kernel(hp_ref, hm_ref, hn_ref, wg_ref, wv_ref, cwg_ref, cwv_ref, cbg_ref, cbv_ref, wd_ref,
                x1_ref, gpost_ref, o_ref, src_ref, hx_ref, acc_ref, ug_ref, uv_ref, actf_ref, *, seq):
    i, j = pl.program_id(0), pl.program_id(1)
    tm = hm_ref.shape[0]
    ext = hx_ref.shape[0]
    pitch = ext // F32_SUBLANES
    hal = F32_SUBLANES
    n_slab_in = hm_ref.shape[1] // LANES

    @pl.when(j == 0)
    def _():
        first = (i * tm) % seq == 0
        last = ((i + 1) * tm) % seq == 0
        hp = hp_ref[hp_ref.shape[0] - hal:, :].astype(F32)
        hn = hn_ref[0:hal, :].astype(F32)
        hp = jnp.where(first, jnp.zeros_like(hp), hp)
        hn = jnp.where(last, jnp.zeros_like(hn), hn)
        for l in range(n_slab_in):
            lanes = slice(l * LANES, (l + 1) * LANES)
            src_ref[l, 0:hal, :] = hp[:, lanes]
            src_ref[l, hal:hal + tm, :] = hm_ref[:, lanes].astype(F32)
            src_ref[l, hal + tm:, :] = hn[:, lanes]

        def interleave(w, carry):
            r0 = pl.multiple_of(w * BF16_SUBLANES, BF16_SUBLANES)
            for l in range(n_slab_in):
                a = src_ref[l, pl.ds(2 * w, F32_SUBLANES, stride=pitch), :]
                b = src_ref[l, pl.ds(2 * w + 1, F32_SUBLANES, stride=pitch), :]
                hx_ref[pl.ds(r0, BF16_SUBLANES), l * LANES:(l + 1) * LANES] = (
                    jnp.concatenate([a, b], axis=0).astype(BF16))
            return carry

        lax.fori_loop(0, ext // BF16_SUBLANES, interleave, 0)
        acc_ref[...] = jnp.zeros_like(acc_ref)

    def conv(u, cw, cb):
        body = ext - F32_SUBLANES
        prev = jnp.concatenate([pltpu.roll(u[body:], 1, 0), u[:body]], axis=0)
        nxt = jnp.concatenate([u[F32_SUBLANES:], pltpu.roll(u[:F32_SUBLANES], F32_SUBLANES - 1, 0)], axis=0)
        return cb + cw[0:1] * prev + cw[1:2] * u + cw[2:3] * nxt

    tn = wg_ref.shape[1]
    subs = [slice(c * FFN_SUB, (c + 1) * FFN_SUB) for c in range(tn // FFN_SUB)]
    for cols in subs:
        ug_ref[:, cols] = jnp.dot(hx_ref[...], wg_ref[:, cols], preferred_element_type=F32)
        uv_ref[:, cols] = jnp.dot(hx_ref[...], wv_ref[:, cols], preferred_element_type=F32)
    for c, cols in enumerate(subs):
        act = (jax.nn.gelu(conv(ug_ref[:, cols], cwg_ref[:, cols], cbg_ref[:, cols]), approximate=True)
               * conv(uv_ref[:, cols], cwv_ref[:, cols], cbv_ref[:, cols]))
        for l in range(FFN_SUB // LANES):
            for v in range(pitch):
                actf_ref[c * (FFN_SUB // LANES) + l, pl.ds(v, F32_SUBLANES, stride=pitch), :] = (
                    act[v * F32_SUBLANES:(v + 1) * F32_SUBLANES, l * LANES:(l + 1) * LANES])
        lhs = jnp.concatenate(
            [actf_ref[c * (FFN_SUB // LANES) + l, hal:hal + tm, :] for l in range(FFN_SUB // LANES)],
            axis=1).astype(BF16)
        acc_ref[...] += jnp.dot(lhs, wd_ref[cols, :], preferred_element_type=F32)

    @pl.when(j == pl.num_programs(1) - 1)
    def _():
        o_ref[...] = x1_ref[...] + _rms(acc_ref[...], gpost_ref[...])


def _ffn(hf, x1, w_up, conv_w, conv_b, w_down, gpost, *, seq, tm, tn):
    t, d = x1.shape
    dff = w_down.shape[0]
    assert dff % tn == 0 and tn % FFN_SUB == 0
    nj = dff // tn
    halo = BF16_SUBLANES
    r = tm // halo
    nhb = t // halo
    ext = tm + 2 * F32_SUBLANES
    assert ext % BF16_SUBLANES == 0 and d % LANES == 0
    w_up_blk = w_up.reshape(d, 2 * nj, tn).transpose(1, 0, 2)
    vmem = 2 * (tm + 2 * halo) * d * 2 + ext * d * (4 + 2) + 2 * 2 * d * tn * 2 + 2 * tn * d * 2
    vmem += 2 * tm * d * 4 + 2 * tm * d * 4 + tm * d * 4 + 3 * ext * tn * 4 + 8 * tm * tn * 4
    return pl.pallas_call(
        functools.partial(_ffn_kernel, seq=seq),
        out_shape=jax.ShapeDtypeStruct((t, d), F32),
        grid=(t // tm, nj),
        in_specs=[
            pl.BlockSpec((halo, d), lambda i, j: (jnp.maximum(i * r - 1, 0), 0)),
            pl.BlockSpec((tm, d), lambda i, j: (i, 0)),
            pl.BlockSpec((halo, d), lambda i, j: (jnp.minimum((i + 1) * r, nhb - 1), 0)),
            pl.BlockSpec((None, d, tn), lambda i, j: (j, 0, 0)),
            pl.BlockSpec((None, d, tn), lambda i, j: (j + nj, 0, 0)),
            pl.BlockSpec((CONV_WIDTH, tn), lambda i, j: (0, j)),
            pl.BlockSpec((CONV_WIDTH, tn), lambda i, j: (0, j + nj)),
            pl.BlockSpec((1, tn), lambda i, j: (0, j)),
            pl.BlockSpec((1, tn), lambda i, j: (0, j + nj)),
            pl.BlockSpec((tn, d), lambda i, j: (j, 0)),
            pl.BlockSpec((tm, d), lambda i, j: (i, 0)),
            pl.BlockSpec((1, d), lambda i, j: (0, 0)),
        ],
        out_specs=pl.BlockSpec((tm, d), lambda i, j: (i, 0)),
        scratch_shapes=[pltpu.VMEM((d // LANES, ext, LANES), F32), pltpu.VMEM((ext, d), BF16),
                        pltpu.VMEM((tm, d), F32), pltpu.VMEM((ext, tn), F32), pltpu.VMEM((ext, tn), F32),
                        pltpu.VMEM((tn // LANES, ext, LANES), F32)],
        compiler_params=pltpu.CompilerParams(
            dimension_semantics=("parallel", "arbitrary"), vmem_limit_bytes=_vmem_limit(vmem)),
        name="ffn",
    )(hf, hf, hf, w_up_blk, w_up_blk, conv_w, conv_w, conv_b, conv_b, w_down, x1, gpost)


def kernel(x, norm_mix_pre, w_in, sg_ln_g, sg_ln_b, sg_w, sg_b, attn_sink, w_branch_a, w_branch_b, w_out,
           norm_mix_post, norm_ffn_pre, w_up, conv_w, conv_b, w_down, norm_ffn_post):
    b, s, d = x.shape
    t = b * s
    depth = w_in.shape[0]
    sg_width = SG_GROUPS * SG_GROUP_DIM
    n_in = w_in.shape[-1]
    gate_width = n_in - 2 * sg_width - N_HEADS * HEAD_DIM - 2 * N_KV_HEADS * HEAD_DIM
    assert gate_width == 2 * d
    u_col = gate_width
    q_col = u_col + 2 * sg_width
    kv_col = q_col + N_HEADS * HEAD_DIM
    tabs = _rope_tables(s)
    row = lambda p: p.reshape(1, -1)
    x2 = x.reshape(t, d)
    for l in range(depth):
        z = _in_proj(x2, row(norm_mix_pre[l]), w_in[l].astype(BF16), tabs, seq=s, tm=1024)
        a_out = _sgate(z, row(sg_ln_g[l]), row(sg_ln_b[l]), sg_w[l].astype(BF16), sg_b[l].T,
                       u_blk=u_col // sg_width, tm=512)
        b_out = _attn(z, attn_sink[l], seq=s, q_col=q_col, kv_col=kv_col, tq=512)
        x1, hf = _merge(a_out, b_out, z, x2, w_branch_a[l].astype(BF16), w_branch_b[l].astype(BF16),
                        w_out[l].astype(BF16), row(norm_mix_post[l]), row(norm_ffn_pre[l]), tm=512)
        x2 = _ffn(hf, x1, w_up[l].astype(BF16), conv_w[l], row(conv_b[l]), w_down[l].astype(BF16),
                  row(norm_ffn_post[l]), seq=s, tm=512, tn=512)
    return x2.reshape(b, s, d)
```

```python
import functools

import jax
import jax.numpy as jnp
from jax import lax
from jax.experimental import pallas as pl
from jax.experimental.pallas import tpu as pltpu

F32 = jnp.float32
BF16 = jnp.bfloat16

LANES = 128
F32_SUBLANES = 8
BF16_SUBLANES = 16
VMEM_BYTES_V7X = 64 * 1024 * 1024

SG_GROUPS = 8
SG_GROUP_DIM = 128
SG_CHUNK = 128
N_HEADS = 8
N_KV_HEADS = 2
HEAD_DIM = 128
Q_PER_KV = N_HEADS // N_KV_HEADS
WINDOW = 128
BLOCK = 128
ROPE_THETA = 500000.0
ROT_DIM = HEAD_DIM // 4
ROT_HALF = ROT_DIM // 2
CONV_WIDTH = 3
EPS = 1e-6
MASK_VALUE = -1e30

IN_TN = 512
IN_ROWS = 256
MERGE_ROWS = 128
FFN_SUB = 256


def _vmem_limit(nbytes):
    return int(min(VMEM_BYTES_V7X - (4 << 20), nbytes))


def _rms(x, g):
    return x * lax.rsqrt(jnp.mean(x * x, axis=-1, keepdims=True) + EPS) * g


def _rope_table_kernel(invf_ref, c_ref, sa_ref, sb_ref):
    rows = c_ref.shape[0]
    pos = (lax.broadcasted_iota(jnp.int32, (rows, LANES), 0) + pl.program_id(0) * rows).astype(F32)
    ang = pos * invf_ref[...]
    lane = lax.broadcasted_iota(jnp.int32, (rows, LANES), 1)
    s = jnp.sin(ang)
    c_ref[...] = jnp.cos(ang)
    sa_ref[...] = jnp.where(lane < ROT_HALF, -s, 0.0)
    sb_ref[...] = jnp.where((lane >= ROT_HALF) & (lane < ROT_DIM), s, 0.0)


def _rope_tables(seq):
    inv_freq = ROPE_THETA ** (-jnp.arange(0, ROT_DIM, 2, dtype=F32) / ROT_DIM)
    invf = jnp.concatenate([inv_freq, inv_freq, jnp.zeros((LANES - ROT_DIM,), F32)]).reshape(1, LANES)
    rows = min(seq, 512)
    tab = jax.ShapeDtypeStruct((seq, LANES), F32)
    return pl.pallas_call(
        _rope_table_kernel,
        out_shape=(tab, tab, tab),
        grid=(seq // rows,),
        in_specs=[pl.BlockSpec((1, LANES), lambda i: (0, 0))],
        out_specs=tuple(pl.BlockSpec((rows, LANES), lambda i: (i, 0)) for _ in range(3)),
        name="rope_table",
    )(invf)


def _rope_head(z, c, sa, sb):
    return z * c + pltpu.roll(z, HEAD_DIM - ROT_HALF, 1) * sa + pltpu.roll(z, ROT_HALF, 1) * sb


def _in_proj_kernel(x_ref, g_ref, w_ref, c_ref, sa_ref, sb_ref, o_ref, h_ref, z_ref, *, n_gate, n_uv, n_q):
    j = pl.program_id(1)
    tm = h_ref.shape[0]
    chunks = [slice(r, r + IN_ROWS) for r in range(0, tm, IN_ROWS)]

    @pl.when(j == 0)
    def _():
        h_ref[...] = _rms(x_ref[...], g_ref[...]).astype(BF16)

    def run(epilogue):
        for rows in chunks:
            z_ref[rows, :] = jnp.dot(h_ref[rows, :], w_ref[...], preferred_element_type=F32)
        for rows in chunks:
            epilogue(rows)

    def sigmoid_rows(rows):
        o_ref[rows, :] = (0.5 * jnp.tanh(0.5 * z_ref[rows, :]) + 0.5).astype(BF16)

    def gelu_rows(rows):
        o_ref[rows, :] = jax.nn.gelu(z_ref[rows, :], approximate=True).astype(BF16)

    def rope_rows(rows, n_heads, scale):
        c, sa, sb = c_ref[rows, :], sa_ref[rows, :], sb_ref[rows, :]
        for h in range(n_heads):
            sl = slice(h * HEAD_DIM, (h + 1) * HEAD_DIM)
            r = _rope_head(z_ref[rows, sl], c, sa, sb)
            o_ref[rows, sl] = (r if scale is None else r * scale).astype(BF16)
        if n_heads * HEAD_DIM < IN_TN:
            rest = slice(n_heads * HEAD_DIM, IN_TN)
            o_ref[rows, rest] = z_ref[rows, rest].astype(BF16)

    @pl.when(j < n_gate)
    def _():
        run(sigmoid_rows)

    @pl.when((j >= n_gate) & (j < n_gate + n_uv))
    def _():
        run(gelu_rows)

    @pl.when((j >= n_gate + n_uv) & (j < n_gate + n_uv + n_q))
    def _():
        run(functools.partial(rope_rows, n_heads=IN_TN // HEAD_DIM, scale=HEAD_DIM ** -0.5))

    @pl.when(j >= n_gate + n_uv + n_q)
    def _():
        run(functools.partial(rope_rows, n_heads=N_KV_HEADS, scale=None))


def _in_proj(x2, gain, w_bf, tabs, *, seq, tm):
    t, d = x2.shape
    n_in = w_bf.shape[1]
    sg_width = SG_GROUPS * SG_GROUP_DIM
    q_width = N_HEADS * HEAD_DIM
    kv_width = N_KV_HEADS * HEAD_DIM
    assert IN_TN == 2 * kv_width
    n_uv = 2 * sg_width // IN_TN
    n_q = q_width // IN_TN
    n_kv = 1
    n_gate = (n_in - 2 * sg_width - q_width - 2 * kv_width) // IN_TN
    nj = n_gate + n_uv + n_q + n_kv
    assert nj * IN_TN == n_in
    first_gate_blk = n_uv + n_q + n_kv
    tiles_per_seq = seq // tm
    kern = functools.partial(_in_proj_kernel, n_gate=n_gate, n_uv=n_uv, n_q=n_q)
    tab_spec = pl.BlockSpec((tm, LANES), lambda i, j: (i % tiles_per_seq, 0))
    vmem = 2 * tm * d * 4 + tm * d * 2 + 2 * d * IN_TN * 2 + 2 * tm * IN_TN * 2 + 6 * tm * LANES * 4
    vmem += 6 * tm * IN_TN * 4
    return pl.pallas_call(
        kern,
        out_shape=jax.ShapeDtypeStruct((t, n_in), BF16),
        grid=(t // tm, nj),
        in_specs=[
            pl.BlockSpec((tm, d), lambda i, j: (i, 0)),
            pl.BlockSpec((1, d), lambda i, j: (0, 0)),
            pl.BlockSpec((d, IN_TN), lambda i, j: (0, (j + first_gate_blk) % nj)),
            tab_spec, tab_spec, tab_spec,
        ],
        out_specs=pl.BlockSpec((tm, IN_TN), lambda i, j: (i, j)),
        scratch_shapes=[pltpu.VMEM((tm, d), BF16), pltpu.VMEM((tm, IN_TN), F32)],
        compiler_params=pltpu.CompilerParams(
            dimension_semantics=("parallel", "arbitrary"), vmem_limit_bytes=_vmem_limit(vmem)),
        name="in_proj",
    )(x2, gain, w_bf, *tabs)


def _sgate_kernel(u_ref, v_ref, lng_ref, lnb_ref, w_ref, bt_ref, o_ref):
    tm = u_ref.shape[0]
    v = v_ref[...].astype(F32)
    mu = jnp.mean(v, axis=-1, keepdims=True)
    xc = v - mu
    vn = (xc * lax.rsqrt(jnp.mean(xc * xc, axis=-1, keepdims=True) + EPS) * lng_ref[...]
          + lnb_ref[...]).astype(BF16)
    for c in range(tm // SG_CHUNK):
        rows = slice(c * SG_CHUNK, (c + 1) * SG_CHUNK)
        for g in range(SG_GROUPS):
            cols = slice(g * SG_GROUP_DIM, (g + 1) * SG_GROUP_DIM)
            vz = jnp.dot(w_ref[g], vn[rows, cols], preferred_element_type=F32) + bt_ref[:, g:g + 1]
            o_ref[rows, cols] = (u_ref[rows, cols].astype(F32) * vz).astype(BF16)


def _sgate(z, ln_g, ln_b, w_bf, b_t, *, u_blk, tm):
    t = z.shape[0]
    sg_width = SG_GROUPS * SG_GROUP_DIM
    vmem = 4 * tm * sg_width * 2 + 2 * tm * sg_width * 2 + 4 * tm * sg_width * 4 + (1 << 20)
    return pl.pallas_call(
        _sgate_kernel,
        out_shape=jax.ShapeDtypeStruct((t, sg_width), BF16),
        grid=(t // tm,),
        in_specs=[
            pl.BlockSpec((tm, sg_width), lambda i: (i, u_blk)),
            pl.BlockSpec((tm, sg_width), lambda i: (i, u_blk + 1)),
            pl.BlockSpec((1, sg_width), lambda i: (0, 0)),
            pl.BlockSpec((1, sg_width), lambda i: (0, 0)),
            pl.BlockSpec((SG_GROUPS, SG_CHUNK, SG_CHUNK), lambda i: (0, 0, 0)),
            pl.BlockSpec((SG_CHUNK, SG_GROUPS), lambda i: (0, 0)),
        ],
        out_specs=pl.BlockSpec((tm, sg_width), lambda i: (i, 0)),
        compiler_params=pltpu.CompilerParams(
            dimension_semantics=("parallel",), vmem_limit_bytes=_vmem_limit(vmem)),
        name="sgate",
    )(z, z, ln_g, ln_b, w_bf, b_t)


def _attn_kernel(sink_ref, q_ref, kvp_ref, kvm_ref, kvn_ref, o_ref, k_ref, v_ref, s_ref, p_ref, *, seq):
    tq = q_ref.shape[0]
    kw = N_KV_HEADS * HEAD_DIM
    nrow = tq + 2 * BLOCK
    k_ref[0:BLOCK, :] = kvp_ref[:, 0:kw]
    k_ref[BLOCK:BLOCK + tq, :] = kvm_ref[:, 0:kw]
    k_ref[BLOCK + tq:, :] = kvn_ref[:, 0:kw]
    ones = jnp.ones((nrow, HEAD_DIM), BF16)
    for g in range(N_KV_HEADS):
        vc = slice(kw + g * HEAD_DIM, kw + (g + 1) * HEAD_DIM)
        v_ref[0:BLOCK, 2 * g * HEAD_DIM:(2 * g + 1) * HEAD_DIM] = kvp_ref[:, vc]
        v_ref[BLOCK:BLOCK + tq, 2 * g * HEAD_DIM:(2 * g + 1) * HEAD_DIM] = kvm_ref[:, vc]
        v_ref[BLOCK + tq:, 2 * g * HEAD_DIM:(2 * g + 1) * HEAD_DIM] = kvn_ref[:, vc]
        v_ref[:, (2 * g + 1) * HEAD_DIM:(2 * g + 2) * HEAD_DIM] = ones
    blk0 = (pl.program_id(0) * tq) % seq // BLOCK
    nq = Q_PER_KV * BLOCK
    qi = lax.broadcasted_iota(jnp.int32, (nq, 3 * BLOCK), 0) & (BLOCK - 1)
    kj = lax.broadcasted_iota(jnp.int32, (nq, 3 * BLOCK), 1)
    rel = kj - BLOCK - qi
    band = (rel >= -WINDOW) & (rel <= WINDOW)
    units = [(qb, g) for qb in range(tq // BLOCK) for g in range(N_KV_HEADS)]

    def heads_of(g):
        return [g * Q_PER_KV + i for i in range(Q_PER_KV)]

    def scores(u):
        qb, g = units[u]
        rows = slice(qb * BLOCK, (qb + 1) * BLOCK)
        qg = jnp.concatenate([q_ref[rows, h * HEAD_DIM:(h + 1) * HEAD_DIM] for h in heads_of(g)], axis=0)
        kwin = k_ref[qb * BLOCK:(qb + 3) * BLOCK, g * HEAD_DIM:(g + 1) * HEAD_DIM]
        s_ref[u % 3] = lax.dot_general(qg, kwin, (((1,), (1,)), ((), ())), preferred_element_type=F32)

    def softmax_num(u):
        qb, g = units[u]
        k_pos = (blk0 + qb - 1) * BLOCK + kj
        valid = band & (k_pos >= 0) & (k_pos < seq)
        s = jnp.where(valid, s_ref[u % 3], MASK_VALUE)
        sink = jnp.concatenate([jnp.full((BLOCK, 1), sink_ref[h], F32) for h in heads_of(g)], axis=0)
        m = jnp.maximum(jnp.max(s, axis=-1, keepdims=True), sink)
        p_ref[u % 2] = jnp.exp(s - m).astype(BF16)
        return jnp.exp(sink - m)

    def weighted_sum(u, e_sink):
        qb, g = units[u]
        rows = slice(qb * BLOCK, (qb + 1) * BLOCK)
        vwin = v_ref[qb * BLOCK:(qb + 3) * BLOCK, 2 * g * HEAD_DIM:(2 * g + 2) * HEAD_DIM]
        ox = jnp.dot(p_ref[u % 2], vwin, preferred_element_type=F32)
        o = ox[:, 0:HEAD_DIM] / (ox[:, HEAD_DIM:] + e_sink)
        for i, h in enumerate(heads_of(g)):
            o_ref[rows, h * HEAD_DIM:(h + 1) * HEAD_DIM] = o[i * BLOCK:(i + 1) * BLOCK].astype(BF16)

    scores(0)
    scores(1)
    for u in range(len(units)):
        e_sink = softmax_num(u)
        if u + 2 < len(units):
            scores(u + 2)
        weighted_sum(u, e_sink)


def _attn(z, sink, *, seq, q_col, kv_col, tq):
    t = z.shape[0]
    q_width = N_HEADS * HEAD_DIM
    kvw = 2 * N_KV_HEADS * HEAD_DIM
    assert q_col % q_width == 0 and kv_col % kvw == 0
    qb_, kb_ = q_col // q_width, kv_col // kvw
    r = tq // BLOCK
    nblk = t // BLOCK
    vmem = 4 * tq * q_width * 2 + 2 * (tq + 2 * BLOCK) * kvw * 2 + (tq + 2 * BLOCK) * kvw * 2
    vmem += 16 * Q_PER_KV * BLOCK * 3 * BLOCK * 4
    return pl.pallas_call(
        functools.partial(_attn_kernel, seq=seq),
        out_shape=jax.ShapeDtypeStruct((t, q_width), BF16),
        grid=(t // tq,),
        in_specs=[
            pl.BlockSpec(memory_space=pltpu.SMEM),
            pl.BlockSpec((tq, q_width), lambda i: (i, qb_)),
            pl.BlockSpec((BLOCK, kvw), lambda i: (jnp.maximum(i * r - 1, 0), kb_)),
            pl.BlockSpec((tq, kvw), lambda i: (i, kb_)),
            pl.BlockSpec((BLOCK, kvw), lambda i: (jnp.minimum((i + 1) * r, nblk - 1), kb_)),
        ],
        out_specs=pl.BlockSpec((tq, q_width), lambda i: (i, 0)),
        scratch_shapes=[pltpu.VMEM((tq + 2 * BLOCK, kvw // 2), BF16),
                        pltpu.VMEM((tq + 2 * BLOCK, kvw), BF16),
                        pltpu.VMEM((3, Q_PER_KV * BLOCK, 3 * BLOCK), F32),
                        pltpu.VMEM((2, Q_PER_KV * BLOCK, 3 * BLOCK), BF16)],
        compiler_params=pltpu.CompilerParams(
            dimension_semantics=("parallel",), vmem_limit_bytes=_vmem_limit(vmem)),
        name="attn",
    )(sink, z, z, z, z)


def _merge_kernel(a_ref, b_ref, ga_ref, gb_ref, x_ref, wa_ref, wb_ref, wo_ref, gpost_ref, gpre_ref,
                  x1_ref, hf_ref, m_ref, y_ref):
    tm = a_ref.shape[0]
    chunks = [slice(r, r + MERGE_ROWS) for r in range(0, tm, MERGE_ROWS)]
    for rows in chunks:
        pa = jnp.dot(a_ref[rows, :], wa_ref[...], preferred_element_type=F32)
        pb = jnp.dot(b_ref[rows, :], wb_ref[...], preferred_element_type=F32)
        m_ref[rows, :] = (ga_ref[rows, :].astype(F32) * pa + gb_ref[rows, :].astype(F32) * pb).astype(BF16)
    for rows in chunks:
        y_ref[rows, :] = jnp.dot(m_ref[rows, :], wo_ref[...], preferred_element_type=F32)
    for rows in chunks:
        x1 = x_ref[rows, :] + _rms(y_ref[rows, :], gpost_ref[...])
        x1_ref[rows, :] = x1
        hf_ref[rows, :] = _rms(x1, gpre_ref[...]).astype(BF16)


def _merge(a, b, z, x2, wa, wb, wo, gpost, gpre, *, tm):
    t, d = x2.shape
    ka, kb = a.shape[1], b.shape[1]
    resident = dict(pipeline_mode=pl.Buffered(1))
    vmem = 2 * tm * (ka + kb) * 2 + 4 * tm * d * 2 + 2 * tm * d * 4 + (ka + kb + d) * d * 2
    vmem += 2 * tm * d * 4 + 2 * tm * d * 2 + 6 * tm * d * 4
    return pl.pallas_call(
        _merge_kernel,
        out_shape=(jax.ShapeDtypeStruct((t, d), F32), jax.ShapeDtypeStruct((t, d), BF16)),
        grid=(t // tm,),
        in_specs=[
            pl.BlockSpec((tm, ka), lambda i: (i, 0)),
            pl.BlockSpec((tm, kb), lambda i: (i, 0)),
            pl.BlockSpec((tm, d), lambda i: (i, 0)),
            pl.BlockSpec((tm, d), lambda i: (i, 1)),
            pl.BlockSpec((tm, d), lambda i: (i, 0)),
            pl.BlockSpec((ka, d), lambda i: (0, 0), **resident),
            pl.BlockSpec((kb, d), lambda i: (0, 0), **resident),
            pl.BlockSpec((d, d), lambda i: (0, 0), **resident),
            pl.BlockSpec((1, d), lambda i: (0, 0)),
            pl.BlockSpec((1, d), lambda i: (0, 0)),
        ],
        out_specs=(pl.BlockSpec((tm, d), lambda i: (i, 0)), pl.BlockSpec((tm, d), lambda i: (i, 0))),
        scratch_shapes=[pltpu.VMEM((tm, d), BF16), pltpu.VMEM((tm, d), F32)],
        compiler_params=pltpu.CompilerParams(
            dimension_semantics=("parallel",), vmem_limit_bytes=_vmem_limit(vmem)),
        name="merge",
    )(a, b, z, z, x2, wa, wb, wo, gpost, gpre)


def _ffn_kernel(hp_ref, hm_ref, hn_ref, wg_ref, wv_ref, cwg_ref, cwv_ref, cbg_ref, cbv_ref, wd_ref,
                x1_ref, gpost_ref, o_ref, src_ref, hx_ref, acc_ref, ug_ref, uv_ref, actf_ref, *, seq):
    i, j = pl.program_id(0), pl.program_id(1)
    tm = hm_ref.shape[0]
    ext = hx_ref.shape[0]
    pitch = ext // F32_SUBLANES
    hal = F32_SUBLANES
    n_slab_in = hm_ref.shape[1] // LANES

    @pl.when(j == 0)
    def _():
        first = (i * tm) % seq == 0
        last = ((i + 1) * tm) % seq == 0
        hp = hp_ref[hp_ref.shape[0] - hal:, :].astype(F32)
        hn = hn_ref[0:hal, :].astype(F32)
        hp = jnp.where(first, jnp.zeros_like(hp), hp)
        hn = jnp.where(last, jnp.zeros_like(hn), hn)
        for l in range(n_slab_in):
            lanes = slice(l * LANES, (l + 1) * LANES)
            src_ref[l, 0:hal, :] = hp[:, lanes]
            src_ref[l, hal:hal + tm, :] = hm_ref[:, lanes].astype(F32)
            src_ref[l, hal + tm:, :] = hn[:, lanes]

        def interleave(w, carry):
            r0 = pl.multiple_of(w * BF16_SUBLANES, BF16_SUBLANES)
            for l in range(n_slab_in):
                a = src_ref[l, pl.ds(2 * w, F32_SUBLANES, stride=pitch), :]
                b = src_ref[l, pl.ds(2 * w + 1, F32_SUBLANES, stride=pitch), :]
                hx_ref[pl.ds(r0, BF16_SUBLANES), l * LANES:(l + 1) * LANES] = (
                    jnp.concatenate([a, b], axis=0).astype(BF16))
            return carry

        lax.fori_loop(0, ext // BF16_SUBLANES, interleave, 0)
        acc_ref[...] = jnp.zeros_like(acc_ref)

    def conv(u, cw, cb):
        body = ext - F32_SUBLANES
        prev = jnp.concatenate([pltpu.roll(u[body:], 1, 0), u[:body]], axis=0)
        nxt = jnp.concatenate([u[F32_SUBLANES:], pltpu.roll(u[:F32_SUBLANES], F32_SUBLANES - 1, 0)], axis=0)
        return cb + cw[0:1] * prev + cw[1:2] * u + cw[2:3] * nxt

    tn = wg_ref.shape[1]
    subs = [slice(c * FFN_SUB, (c + 1) * FFN_SUB) for c in range(tn // FFN_SUB)]
    for cols in subs:
        ug_ref[:, cols] = jnp.dot(hx_ref[...], wg_ref[:, cols], preferred_element_type=F32)
        uv_ref[:, cols] = jnp.dot(hx_ref[...], wv_ref[:, cols], preferred_element_type=F32)
    for c, cols in enumerate(subs):
        act = (jax.nn.gelu(conv(ug_ref[:, cols], cwg_ref[:, cols], cbg_ref[:, cols]), approximate=True)
               * conv(uv_ref[:, cols], cwv_ref[:, cols], cbv_ref[:, cols]))
        for l in range(FFN_SUB // LANES):
            for v in range(pitch):
                actf_ref[c * (FFN_SUB // LANES) + l, pl.ds(v, F32_SUBLANES, stride=pitch), :] = (
                    act[v * F32_SUBLANES:(v + 1) * F32_SUBLANES, l * LANES:(l + 1) * LANES])
        lhs = jnp.concatenate(
            [actf_ref[c * (FFN_SUB // LANES) + l, hal:hal + tm, :] for l in range(FFN_SUB // LANES)],
            axis=1).astype(BF16)
        acc_ref[...] += jnp.dot(lhs, wd_ref[cols, :], preferred_element_type=F32)

    @pl.when(j == pl.num_programs(1) - 1)
    def _():
        o_ref[...] = x1_ref[...] + _rms(acc_ref[...], gpost_ref[...])


def _ffn(hf, x1, w_up, conv_w, conv_b, w_down, gpost, *, seq, tm, tn):
    t, d = x1.shape
    dff = w_down.shape[0]
    assert dff % tn == 0 and tn % FFN_SUB == 0
    nj = dff // tn
    halo = BF16_SUBLANES
    r = tm // halo
    nhb = t // halo
    ext = tm + 2 * F32_SUBLANES
    assert ext % BF16_SUBLANES == 0 and d % LANES == 0
    vmem = 2 * (tm + 2 * halo) * d * 2 + ext * d * (4 + 2) + 2 * 2 * d * tn * 2 + 2 * tn * d * 2
    vmem += 2 * tm * d * 4 + 2 * tm * d * 4 + tm * d * 4 + 3 * ext * tn * 4 + 8 * tm * tn * 4
    return pl.pallas_call(
        functools.partial(_ffn_kernel, seq=seq),
        out_shape=jax.ShapeDtypeStruct((t, d), F32),
        grid=(t // tm, nj),
        in_specs=[
            pl.BlockSpec((halo, d), lambda i, j: (jnp.maximum(i * r - 1, 0), 0)),
            pl.BlockSpec((tm, d), lambda i, j: (i, 0)),
            pl.BlockSpec((halo, d), lambda i, j: (jnp.minimum((i + 1) * r, nhb - 1), 0)),
            pl.BlockSpec((d, tn), lambda i, j: (0, j)),
            pl.BlockSpec((d, tn), lambda i, j: (0, j + nj)),
            pl.BlockSpec((CONV_WIDTH, tn), lambda i, j: (0, j)),
            pl.BlockSpec((CONV_WIDTH, tn), lambda i, j: (0, j + nj)),
            pl.BlockSpec((1, tn), lambda i, j: (0, j)),
            pl.BlockSpec((1, tn), lambda i, j: (0, j + nj)),
            pl.BlockSpec((tn, d), lambda i, j: (j, 0)),
            pl.BlockSpec((tm, d), lambda i, j: (i, 0)),
            pl.BlockSpec((1, d), lambda i, j: (0, 0)),
        ],
        out_specs=pl.BlockSpec((tm, d), lambda i, j: (i, 0)),
        scratch_shapes=[pltpu.VMEM((d // LANES, ext, LANES), F32), pltpu.VMEM((ext, d), BF16),
                        pltpu.VMEM((tm, d), F32), pltpu.VMEM((ext, tn), F32), pltpu.VMEM((ext, tn), F32),
                        pltpu.VMEM((tn // LANES, ext, LANES), F32)],
        compiler_params=pltpu.CompilerParams(
            dimension_semantics=("parallel", "arbitrary"), vmem_limit_bytes=_vmem_limit(vmem)),
        name="ffn",
    )(hf, hf, hf, w_up, w_up, conv_w, conv_w, conv_b, conv_b, w_down, x1, gpost)


def kernel(x, norm_mix_pre, w_in, sg_ln_g, sg_ln_b, sg_w, sg_b, attn_sink, w_branch_a, w_branch_b, w_out,
           norm_mix_post, norm_ffn_pre, w_up, conv_w, conv_b, w_down, norm_ffn_post):
    b, s, d = x.shape
    t = b * s
    depth = w_in.shape[0]
    sg_width = SG_GROUPS * SG_GROUP_DIM
    n_in = w_in.shape[-1]
    gate_width = n_in - 2 * sg_width - N_HEADS * HEAD_DIM - 2 * N_KV_HEADS * HEAD_DIM
    assert gate_width == 2 * d
    u_col = gate_width
    q_col = u_col + 2 * sg_width
    kv_col = q_col + N_HEADS * HEAD_DIM
    tabs = _rope_tables(s)
    row = lambda p: p.reshape(1, -1)
    x2 = x.reshape(t, d)
    for l in range(depth):
        z = _in_proj(x2, row(norm_mix_pre[l]), w_in[l].astype(BF16), tabs, seq=s, tm=1024)
        a_out = _sgate(z, row(sg_ln_g[l]), row(sg_ln_b[l]), sg_w[l].astype(BF16), sg_b[l].T,
                       u_blk=u_col // sg_width, tm=512)
        b_out = _attn(z, attn_sink[l], seq=s, q_col=q_col, kv_col=kv_col, tq=512)
        x1, hf = _merge(a_out, b_out, z, x2, w_branch_a[l].astype(BF16), w_branch_b[l].astype(BF16),
                        w_out[l].astype(BF16), row(norm_mix_post[l]), row(norm_ffn_pre[l]), tm=512)
        x2 = _ffn(hf, x1, w_up[l].astype(BF16), conv_w[l], row(conv_b[l]), w_down[l].astype(BF16),
                  row(norm_ffn_post[l]), seq=s, tm=512, tn=512)
    return x2.reshape(b, s, d)
```

```python
import functools

import jax
import jax.numpy as jnp
from jax import lax
from jax.experimental import pallas as pl
from jax.experimental.pallas import tpu as pltpu

F32 = jnp.float32
BF16 = jnp.bfloat16

LANES = 128
F32_SUBLANES = 8
BF16_SUBLANES = 16
VMEM_BYTES_V7X = 64 * 1024 * 1024

SG_GROUPS = 8
SG_GROUP_DIM = 128
SG_CHUNK = 128
N_HEADS = 8
N_KV_HEADS = 2
HEAD_DIM = 128
Q_PER_KV = N_HEADS // N_KV_HEADS
WINDOW = 128
BLOCK = 128
ROPE_THETA = 500000.0
ROT_DIM = HEAD_DIM // 4
ROT_HALF = ROT_DIM // 2
CONV_WIDTH = 3
EPS = 1e-6
MASK_VALUE = -1e30

IN_TN = 512
IN_GROUP = 3
IN_ROWS = 256
MERGE_ROWS = 128
DOWN_ROWS = 128
FFN_SUB = 256


def _vmem_limit(nbytes):
    return int(min(VMEM_BYTES_V7X - (4 << 20), nbytes))


def _rms(x, g):
    return x * lax.rsqrt(jnp.mean(x * x, axis=-1, keepdims=True) + EPS) * g


def _rope_table_kernel(invf_ref, c_ref, sa_ref, sb_ref):
    rows = c_ref.shape[0]
    pos = (lax.broadcasted_iota(jnp.int32, (rows, LANES), 0) + pl.program_id(0) * rows).astype(F32)
    ang = pos * invf_ref[...]
    lane = lax.broadcasted_iota(jnp.int32, (rows, LANES), 1)
    s = jnp.sin(ang)
    c_ref[...] = jnp.cos(ang)
    sa_ref[...] = jnp.where(lane < ROT_HALF, -s, 0.0)
    sb_ref[...] = jnp.where((lane >= ROT_HALF) & (lane < ROT_DIM), s, 0.0)


def _rope_tables(seq):
    inv_freq = ROPE_THETA ** (-jnp.arange(0, ROT_DIM, 2, dtype=F32) / ROT_DIM)
    invf = jnp.concatenate([inv_freq, inv_freq, jnp.zeros((LANES - ROT_DIM,), F32)]).reshape(1, LANES)
    rows = min(seq, 512)
    tab = jax.ShapeDtypeStruct((seq, LANES), F32)
    return pl.pallas_call(
        _rope_table_kernel,
        out_shape=(tab, tab, tab),
        grid=(seq // rows,),
        in_specs=[pl.BlockSpec((1, LANES), lambda i: (0, 0))],
        out_specs=tuple(pl.BlockSpec((rows, LANES), lambda i: (i, 0)) for _ in range(3)),
        name="rope_table",
    )(invf)


def _rope_head(z, c, sa, sb):
    return z * c + pltpu.roll(z, HEAD_DIM - ROT_HALF, 1) * sa + pltpu.roll(z, ROT_HALF, 1) * sb


def _in_proj_kernel(x_ref, g_ref, *refs, kinds):
    w_refs = refs[:IN_GROUP]
    c_ref, sa_ref, sb_ref, o_ref, h_ref, z_ref = refs[IN_GROUP:]
    j = pl.program_id(1)
    tm = h_ref.shape[0]
    chunks = [slice(r, r + IN_ROWS) for r in range(0, tm, IN_ROWS)]

    @pl.when(j == 0)
    def _():
        h_ref[...] = _rms(x_ref[...], g_ref[...]).astype(BF16)

    def epilogue(kind, rows, cols):
        if kind == "sigmoid":
            o_ref[rows, cols] = (0.5 * jnp.tanh(0.5 * z_ref[rows, cols]) + 0.5).astype(BF16)
        elif kind == "gelu":
            o_ref[rows, cols] = jax.nn.gelu(z_ref[rows, cols], approximate=True).astype(BF16)
        else:
            n_heads, scale = (IN_TN // HEAD_DIM, HEAD_DIM ** -0.5) if kind == "rope_q" else (N_KV_HEADS, None)
            c, sa, sb = c_ref[rows, :], sa_ref[rows, :], sb_ref[rows, :]
            for h in range(n_heads):
                sl = slice(cols.start + h * HEAD_DIM, cols.start + (h + 1) * HEAD_DIM)
                r = _rope_head(z_ref[rows, sl], c, sa, sb)
                o_ref[rows, sl] = (r if scale is None else r * scale).astype(BF16)
            if n_heads * HEAD_DIM < IN_TN:
                rest = slice(cols.start + n_heads * HEAD_DIM, cols.stop)
                o_ref[rows, rest] = z_ref[rows, rest].astype(BF16)

    n_steps = len(kinds) // IN_GROUP
    for step in range(n_steps):
        step_kinds = kinds[step * IN_GROUP:(step + 1) * IN_GROUP]
        if step > 0 and step_kinds == kinds[(step - 1) * IN_GROUP:step * IN_GROUP]:
            continue
        last = step
        while last + 1 < n_steps and kinds[(last + 1) * IN_GROUP:(last + 2) * IN_GROUP] == step_kinds:
            last += 1

        @pl.when((j >= step) & (j <= last))
        def _(step_kinds=step_kinds):
            for rows in chunks:
                for b in range(IN_GROUP):
                    z_ref[rows, b * IN_TN:(b + 1) * IN_TN] = jnp.dot(
                        h_ref[rows, :], w_refs[b][...], preferred_element_type=F32)
            for rows in chunks:
                for b in range(IN_GROUP):
                    epilogue(step_kinds[b], rows, slice(b * IN_TN, (b + 1) * IN_TN))


def _in_proj(x2, gain, w_bf, tabs, *, seq, tm):
    t, d = x2.shape
    n_in = w_bf.shape[1]
    sg_width = SG_GROUPS * SG_GROUP_DIM
    q_width = N_HEADS * HEAD_DIM
    kv_width = N_KV_HEADS * HEAD_DIM
    assert IN_TN == 2 * kv_width
    n_uv = 2 * sg_width // IN_TN
    n_q = q_width // IN_TN
    n_kv = 1
    n_gate = (n_in - 2 * sg_width - q_width - 2 * kv_width) // IN_TN
    nb = n_gate + n_uv + n_q + n_kv
    assert nb * IN_TN == n_in and nb % IN_GROUP == 0
    kinds = ("sigmoid",) * n_gate + ("gelu",) * n_uv + ("rope_q",) * n_q + ("rope_kv",) * n_kv
    first_gate_blk = n_uv + n_q + n_kv
    tiles_per_seq = seq // tm
    tab_spec = pl.BlockSpec((tm, LANES), lambda i, j: (i % tiles_per_seq, 0))
    w_specs = [pl.BlockSpec((d, IN_TN), lambda i, j, b=b: (0, (j * IN_GROUP + b + first_gate_blk) % nb))
               for b in range(IN_GROUP)]
    tn = IN_GROUP * IN_TN
    vmem = 2 * tm * d * 4 + tm * d * 2 + 2 * d * tn * 2 + 2 * tm * tn * 2 + 6 * tm * LANES * 4
    vmem += tm * tn * 4 + 8 * IN_ROWS * tn * 4
    return pl.pallas_call(
        functools.partial(_in_proj_kernel, kinds=kinds),
        out_shape=jax.ShapeDtypeStruct((t, n_in), BF16),
        grid=(t // tm, nb // IN_GROUP),
        in_specs=[
            pl.BlockSpec((tm, d), lambda i, j: (i, 0)),
            pl.BlockSpec((1, d), lambda i, j: (0, 0)),
            *w_specs,
            tab_spec, tab_spec, tab_spec,
        ],
        out_specs=pl.BlockSpec((tm, tn), lambda i, j: (i, j)),
        scratch_shapes=[pltpu.VMEM((tm, d), BF16), pltpu.VMEM((tm, tn), F32)],
        compiler_params=pltpu.CompilerParams(
            dimension_semantics=("parallel", "arbitrary"), vmem_limit_bytes=_vmem_limit(vmem)),
        name="in_proj",
    )(x2, gain, *([w_bf] * IN_GROUP), *tabs)


def _sgate_kernel(u_ref, v_ref, lng_ref, lnb_ref, w_ref, bt_ref, o_ref):
    tm = u_ref.shape[0]
    v = v_ref[...].astype(F32)
    mu = jnp.mean(v, axis=-1, keepdims=True)
    xc = v - mu
    vn = (xc * lax.rsqrt(jnp.mean(xc * xc, axis=-1, keepdims=True) + EPS) * lng_ref[...]
          + lnb_ref[...]).astype(BF16)
    for c in range(tm // SG_CHUNK):
        rows = slice(c * SG_CHUNK, (c + 1) * SG_CHUNK)
        for g in range(SG_GROUPS):
            cols = slice(g * SG_GROUP_DIM, (g + 1) * SG_GROUP_DIM)
            vz = jnp.dot(w_ref[g], vn[rows, cols], preferred_element_type=F32) + bt_ref[:, g:g + 1]
            o_ref[rows, cols] = (u_ref[rows, cols].astype(F32) * vz).astype(BF16)


def _sgate(z, ln_g, ln_b, w_bf, b_t, *, u_blk, tm):
    t = z.shape[0]
    sg_width = SG_GROUPS * SG_GROUP_DIM
    vmem = 4 * tm * sg_width * 2 + 2 * tm * sg_width * 2 + 4 * tm * sg_width * 4 + (1 << 20)
    return pl.pallas_call(
        _sgate_kernel,
        out_shape=jax.ShapeDtypeStruct((t, sg_width), BF16),
        grid=(t // tm,),
        in_specs=[
            pl.BlockSpec((tm, sg_width), lambda i: (i, u_blk)),
            pl.BlockSpec((tm, sg_width), lambda i: (i, u_blk + 1)),
            pl.BlockSpec((1, sg_width), lambda i: (0, 0)),
            pl.BlockSpec((1, sg_width), lambda i: (0, 0)),
            pl.BlockSpec((SG_GROUPS, SG_CHUNK, SG_CHUNK), lambda i: (0, 0, 0)),
            pl.BlockSpec((SG_CHUNK, SG_GROUPS), lambda i: (0, 0)),
        ],
        out_specs=pl.BlockSpec((tm, sg_width), lambda i: (i, 0)),
        compiler_params=pltpu.CompilerParams(
            dimension_semantics=("parallel",), vmem_limit_bytes=_vmem_limit(vmem)),
        name="sgate",
    )(z, z, ln_g, ln_b, w_bf, b_t)


def _attn_kernel(sink_ref, q_ref, kvp_ref, kvm_ref, kvn_ref, o_ref, k_ref, v_ref, s_ref, p_ref, *, seq):
    tq = q_ref.shape[0]
    kw = N_KV_HEADS * HEAD_DIM
    nrow = tq + 2 * BLOCK
    k_ref[0:BLOCK, :] = kvp_ref[:, 0:kw]
    k_ref[BLOCK:BLOCK + tq, :] = kvm_ref[:, 0:kw]
    k_ref[BLOCK + tq:, :] = kvn_ref[:, 0:kw]
    ones = jnp.ones((nrow, HEAD_DIM), BF16)
    for g in range(N_KV_HEADS):
        vc = slice(kw + g * HEAD_DIM, kw + (g + 1) * HEAD_DIM)
        v_ref[0:BLOCK, 2 * g * HEAD_DIM:(2 * g + 1) * HEAD_DIM] = kvp_ref[:, vc]
        v_ref[BLOCK:BLOCK + tq, 2 * g * HEAD_DIM:(2 * g + 1) * HEAD_DIM] = kvm_ref[:, vc]
        v_ref[BLOCK + tq:, 2 * g * HEAD_DIM:(2 * g + 1) * HEAD_DIM] = kvn_ref[:, vc]
        v_ref[:, (2 * g + 1) * HEAD_DIM:(2 * g + 2) * HEAD_DIM] = ones
    blk0 = (pl.program_id(0) * tq) % seq // BLOCK
    nq = Q_PER_KV * BLOCK
    qi = lax.broadcasted_iota(jnp.int32, (nq, 3 * BLOCK), 0) & (BLOCK - 1)
    kj = lax.broadcasted_iota(jnp.int32, (nq, 3 * BLOCK), 1)
    rel = kj - BLOCK - qi
    band = (rel >= -WINDOW) & (rel <= WINDOW)
    units = [(qb, g) for qb in range(tq // BLOCK) for g in range(N_KV_HEADS)]

    def heads_of(g):
        return [g * Q_PER_KV + i for i in range(Q_PER_KV)]

    def scores(u):
        qb, g = units[u]
        rows = slice(qb * BLOCK, (qb + 1) * BLOCK)
        qg = jnp.concatenate([q_ref[rows, h * HEAD_DIM:(h + 1) * HEAD_DIM] for h in heads_of(g)], axis=0)
        kwin = k_ref[qb * BLOCK:(qb + 3) * BLOCK, g * HEAD_DIM:(g + 1) * HEAD_DIM]
        s_ref[u % 3] = lax.dot_general(qg, kwin, (((1,), (1,)), ((), ())), preferred_element_type=F32)

    def softmax_num(u):
        qb, g = units[u]
        k_pos = (blk0 + qb - 1) * BLOCK + kj
        valid = band & (k_pos >= 0) & (k_pos < seq)
        s = jnp.where(valid, s_ref[u % 3], MASK_VALUE)
        sink = jnp.concatenate([jnp.full((BLOCK, 1), sink_ref[h], F32) for h in heads_of(g)], axis=0)
        m = jnp.maximum(jnp.max(s, axis=-1, keepdims=True), sink)
        p_ref[u % 2] = jnp.exp(s - m).astype(BF16)
        return jnp.exp(sink - m)

    def weighted_sum(u, e_sink):
        qb, g = units[u]
        rows = slice(qb * BLOCK, (qb + 1) * BLOCK)
        vwin = v_ref[qb * BLOCK:(qb + 3) * BLOCK, 2 * g * HEAD_DIM:(2 * g + 2) * HEAD_DIM]
        ox = jnp.dot(p_ref[u % 2], vwin, preferred_element_type=F32)
        o = ox[:, 0:HEAD_DIM] / (ox[:, HEAD_DIM:] + e_sink)
        for i, h in enumerate(heads_of(g)):
            o_ref[rows, h * HEAD_DIM:(h + 1) * HEAD_DIM] = o[i * BLOCK:(i + 1) * BLOCK].astype(BF16)

    scores(0)
    scores(1)
    for u in range(len(units)):
        e_sink = softmax_num(u)
        if u + 2 < len(units):
            scores(u + 2)
        weighted_sum(u, e_sink)


def _attn(z, sink, *, seq, q_col, kv_col, tq):
    t = z.shape[0]
    q_width = N_HEADS * HEAD_DIM
    kvw = 2 * N_KV_HEADS * HEAD_DIM
    assert q_col % q_width == 0 and kv_col % kvw == 0
    qb_, kb_ = q_col // q_width, kv_col // kvw
    r = tq // BLOCK
    nblk = t // BLOCK
    vmem = 4 * tq * q_width * 2 + 2 * (tq + 2 * BLOCK) * kvw * 2 + (tq + 2 * BLOCK) * kvw * 2
    vmem += 16 * Q_PER_KV * BLOCK * 3 * BLOCK * 4
    return pl.pallas_call(
        functools.partial(_attn_kernel, seq=seq),
        out_shape=jax.ShapeDtypeStruct((t, q_width), BF16),
        grid=(t // tq,),
        in_specs=[
            pl.BlockSpec(memory_space=pltpu.SMEM),
            pl.BlockSpec((tq, q_width), lambda i: (i, qb_)),
            pl.BlockSpec((BLOCK, kvw), lambda i: (jnp.maximum(i * r - 1, 0), kb_)),
            pl.BlockSpec((tq, kvw), lambda i: (i, kb_)),
            pl.BlockSpec((BLOCK, kvw), lambda i: (jnp.minimum((i + 1) * r, nblk - 1), kb_)),
        ],
        out_specs=pl.BlockSpec((tq, q_width), lambda i: (i, 0)),
        scratch_shapes=[pltpu.VMEM((tq + 2 * BLOCK, kvw // 2), BF16),
                        pltpu.VMEM((tq + 2 * BLOCK, kvw), BF16),
                        pltpu.VMEM((3, Q_PER_KV * BLOCK, 3 * BLOCK), F32),
                        pltpu.VMEM((2, Q_PER_KV * BLOCK, 3 * BLOCK), BF16)],
        compiler_params=pltpu.CompilerParams(
            dimension_semantics=("parallel",), vmem_limit_bytes=_vmem_limit(vmem)),
        name="attn",
    )(sink, z, z, z, z)


def _merge_kernel(a_ref, b_ref, ga_ref, gb_ref, x_ref, wa_ref, wb_ref, wo_ref, gpost_ref, gpre_ref,
                  x1_ref, hf_ref, m_ref, y_ref):
    tm = a_ref.shape[0]
    chunks = [slice(r, r + MERGE_ROWS) for r in range(0, tm, MERGE_ROWS)]
    for rows in chunks:
        pa = jnp.dot(a_ref[rows, :], wa_ref[...], preferred_element_type=F32)
        pb = jnp.dot(b_ref[rows, :], wb_ref[...], preferred_element_type=F32)
        m_ref[rows, :] = (ga_ref[rows, :].astype(F32) * pa + gb_ref[rows, :].astype(F32) * pb).astype(BF16)
    for rows in chunks:
        y_ref[rows, :] = jnp.dot(m_ref[rows, :], wo_ref[...], preferred_element_type=F32)
    for rows in chunks:
        x1 = x_ref[rows, :] + _rms(y_ref[rows, :], gpost_ref[...])
        x1_ref[rows, :] = x1
        hf_ref[rows, :] = _rms(x1, gpre_ref[...]).astype(BF16)


def _merge(a, b, z, x2, wa, wb, wo, gpost, gpre, *, tm):
    t, d = x2.shape
    ka, kb = a.shape[1], b.shape[1]
    resident = dict(pipeline_mode=pl.Buffered(1))
    vmem = 2 * tm * (ka + kb) * 2 + 4 * tm * d * 2 + 2 * tm * d * 4 + (ka + kb + d) * d * 2
    vmem += 2 * tm * d * 4 + 2 * tm * d * 2 + 6 * tm * d * 4
    return pl.pallas_call(
        _merge_kernel,
        out_shape=(jax.ShapeDtypeStruct((t, d), F32), jax.ShapeDtypeStruct((t, d), BF16)),
        grid=(t // tm,),
        in_specs=[
            pl.BlockSpec((tm, ka), lambda i: (i, 0)),
            pl.BlockSpec((tm, kb), lambda i: (i, 0)),
            pl.BlockSpec((tm, d), lambda i: (i, 0)),
            pl.BlockSpec((tm, d), lambda i: (i, 1)),
            pl.BlockSpec((tm, d), lambda i: (i, 0)),
            pl.BlockSpec((ka, d), lambda i: (0, 0), **resident),
            pl.BlockSpec((kb, d), lambda i: (0, 0), **resident),
            pl.BlockSpec((d, d), lambda i: (0, 0), **resident),
            pl.BlockSpec((1, d), lambda i: (0, 0)),
            pl.BlockSpec((1, d), lambda i: (0, 0)),
        ],
        out_specs=(pl.BlockSpec((tm, d), lambda i: (i, 0)), pl.BlockSpec((tm, d), lambda i: (i, 0))),
        scratch_shapes=[pltpu.VMEM((tm, d), BF16), pltpu.VMEM((tm, d), F32)],
        compiler_params=pltpu.CompilerParams(
            dimension_semantics=("parallel",), vmem_limit_bytes=_vmem_limit(vmem)),
        name="merge",
    )(a, b, z, z, x2, wa, wb, wo, gpost, gpre)


def _ffn_up_kernel(hp_ref, hm_ref, hn_ref, wg_ref, wv_ref, cwg_ref, cwv_ref, cbg_ref, cbv_ref, o_ref,
                   src_ref, hx_ref, ug_ref, uv_ref, actf_ref, *, seq):
    i, j = pl.program_id(0), pl.program_id(1)
    tm = hm_ref.shape[0]
    ext = hx_ref.shape[0]
    pitch = ext // F32_SUBLANES
    hal = F32_SUBLANES
    n_slab_in = hm_ref.shape[1] // LANES

    @pl.when(j == 0)
    def _():
        first = (i * tm) % seq == 0
        last = ((i + 1) * tm) % seq == 0
        hp = hp_ref[hp_ref.shape[0] - hal:, :].astype(F32)
        hn = hn_ref[0:hal, :].astype(F32)
        hp = jnp.where(first, jnp.zeros_like(hp), hp)
        hn = jnp.where(last, jnp.zeros_like(hn), hn)
        for l in range(n_slab_in):
            lanes = slice(l * LANES, (l + 1) * LANES)
            src_ref[0:hal, :] = hp[:, lanes]
            src_ref[hal:hal + tm, :] = hm_ref[:, lanes].astype(F32)
            src_ref[hal + tm:, :] = hn[:, lanes]

            def interleave(w, carry, lanes=lanes):
                r0 = pl.multiple_of(w * BF16_SUBLANES, BF16_SUBLANES)
                a = src_ref[pl.ds(2 * w, F32_SUBLANES, stride=pitch), :]
                b = src_ref[pl.ds(2 * w + 1, F32_SUBLANES, stride=pitch), :]
                hx_ref[pl.ds(r0, BF16_SUBLANES), lanes] = jnp.concatenate([a, b], axis=0).astype(BF16)
                return carry

            lax.fori_loop(0, ext // BF16_SUBLANES, interleave, 0, unroll=5)

    def conv(u, cw, cb):
        body = ext - F32_SUBLANES
        prev = jnp.concatenate([pltpu.roll(u[body:], 1, 0), u[:body]], axis=0)
        nxt = jnp.concatenate([u[F32_SUBLANES:], pltpu.roll(u[:F32_SUBLANES], F32_SUBLANES - 1, 0)], axis=0)
        return cb + cw[0:1] * prev + cw[1:2] * u + cw[2:3] * nxt

    tn = wg_ref.shape[1]
    subs = [slice(c * FFN_SUB, (c + 1) * FFN_SUB) for c in range(tn // FFN_SUB)]
    for cols in subs:
        ug_ref[:, cols] = jnp.dot(hx_ref[...], wg_ref[:, cols], preferred_element_type=F32)
        uv_ref[:, cols] = jnp.dot(hx_ref[...], wv_ref[:, cols], preferred_element_type=F32)
    for c, cols in enumerate(subs):
        act = (jax.nn.gelu(conv(ug_ref[:, cols], cwg_ref[:, cols], cbg_ref[:, cols]), approximate=True)
               * conv(uv_ref[:, cols], cwv_ref[:, cols], cbv_ref[:, cols]))
        for l in range(FFN_SUB // LANES):
            slab = c * (FFN_SUB // LANES) + l
            for v in range(pitch):
                actf_ref[slab, pl.ds(v, F32_SUBLANES, stride=pitch), :] = (
                    act[v * F32_SUBLANES:(v + 1) * F32_SUBLANES, l * LANES:(l + 1) * LANES])
            o_ref[:, slab * LANES:(slab + 1) * LANES] = actf_ref[slab, hal:hal + tm, :].astype(BF16)


def _ffn_up(hf, w_up, conv_w, conv_b, *, seq, tm, tn):
    t, d = hf.shape
    dff = w_up.shape[1] // 2
    assert dff % tn == 0 and tn % FFN_SUB == 0
    nj = dff // tn
    halo = BF16_SUBLANES
    r = tm // halo
    nhb = t // halo
    ext = tm + 2 * F32_SUBLANES
    assert ext % BF16_SUBLANES == 0 and d % LANES == 0
    vmem = 2 * (tm + 2 * halo) * d * 2 + ext * LANES * 4 + ext * d * 2 + 2 * 2 * d * tn * 2
    vmem += 2 * tm * tn * 2 + 3 * ext * tn * 4 + 8 * ext * FFN_SUB * 4
    return pl.pallas_call(
        functools.partial(_ffn_up_kernel, seq=seq),
        out_shape=jax.ShapeDtypeStruct((t, dff), BF16),
        grid=(t // tm, nj),
        in_specs=[
            pl.BlockSpec((halo, d), lambda i, j: (jnp.maximum(i * r - 1, 0), 0)),
            pl.BlockSpec((tm, d), lambda i, j: (i, 0)),
            pl.BlockSpec((halo, d), lambda i, j: (jnp.minimum((i + 1) * r, nhb - 1), 0)),
            pl.BlockSpec((d, tn), lambda i, j: (0, j)),
            pl.BlockSpec((d, tn), lambda i, j: (0, j + nj)),
            pl.BlockSpec((CONV_WIDTH, tn), lambda i, j: (0, j)),
            pl.BlockSpec((CONV_WIDTH, tn), lambda i, j: (0, j + nj)),
            pl.BlockSpec((1, tn), lambda i, j: (0, j)),
            pl.BlockSpec((1, tn), lambda i, j: (0, j + nj)),
        ],
        out_specs=pl.BlockSpec((tm, tn), lambda i, j: (i, j)),
        scratch_shapes=[pltpu.VMEM((ext, LANES), F32), pltpu.VMEM((ext, d), BF16),
                        pltpu.VMEM((ext, tn), F32), pltpu.VMEM((ext, tn), F32),
                        pltpu.VMEM((tn // LANES, ext, LANES), F32)],
        compiler_params=pltpu.CompilerParams(
            dimension_semantics=("parallel", "arbitrary"), vmem_limit_bytes=_vmem_limit(vmem)),
        name="ffn_up",
    )(hf, hf, hf, w_up, w_up, conv_w, conv_w, conv_b, conv_b)


def _ffn_down_kernel(a_ref, wd_ref, x1_ref, gpost_ref, o_ref, f_ref):
    i = pl.program_id(0)
    n = pl.num_programs(0) - 1

    @pl.when(i == 0)
    def _():
        f_ref[...] = jnp.zeros_like(f_ref)

    def finish():
        o_ref[...] = x1_ref[...] + _rms(f_ref[...], gpost_ref[...])

    @pl.when(i < n)
    def _():
        f_new = jnp.dot(a_ref[...], wd_ref[...], preferred_element_type=F32)
        finish()
        f_ref[...] = f_new

    @pl.when(i == n)
    def _():
        finish()


def _ffn_down(act, x1, w_down, gpost, *, tm):
    t, d = x1.shape
    dff = w_down.shape[0]
    n = t // tm
    prev = lambda i: (jnp.maximum(i - 1, 0), 0)
    vmem = dff * d * 2 + 2 * tm * dff * 2 + 4 * tm * d * 4 + tm * d * 4 + 8 * tm * d * 4
    return pl.pallas_call(
        _ffn_down_kernel,
        out_shape=jax.ShapeDtypeStruct((t, d), F32),
        grid=(n + 1,),
        in_specs=[
            pl.BlockSpec((tm, dff), lambda i: (jnp.minimum(i, n - 1), 0)),
            pl.BlockSpec((dff, d), lambda i: (0, 0), pipeline_mode=pl.Buffered(1)),
            pl.BlockSpec((tm, d), prev),
            pl.BlockSpec((1, d), lambda i: (0, 0)),
        ],
        out_specs=pl.BlockSpec((tm, d), prev),
        scratch_shapes=[pltpu.VMEM((tm, d), F32)],
        compiler_params=pltpu.CompilerParams(
            dimension_semantics=("arbitrary",), vmem_limit_bytes=_vmem_limit(vmem)),
        name="ffn_down",
    )(act, w_down, x1, gpost)


def kernel(x, norm_mix_pre, w_in, sg_ln_g, sg_ln_b, sg_w, sg_b, attn_sink, w_branch_a, w_branch_b, w_out,
           norm_mix_post, norm_ffn_pre, w_up, conv_w, conv_b, w_down, norm_ffn_post):
    b, s, d = x.shape
    t = b * s
    depth = w_in.shape[0]
    sg_width = SG_GROUPS * SG_GROUP_DIM
    n_in = w_in.shape[-1]
    gate_width = n_in - 2 * sg_width - N_HEADS * HEAD_DIM - 2 * N_KV_HEADS * HEAD_DIM
    assert gate_width == 2 * d
    u_col = gate_width
    q_col = u_col + 2 * sg_width
    kv_col = q_col + N_HEADS * HEAD_DIM
    tabs = _rope_tables(s)
    row = lambda p: p.reshape(1, -1)
    x2 = x.reshape(t, d)
    for l in range(depth):
        z = _in_proj(x2, row(norm_mix_pre[l]), w_in[l].astype(BF16), tabs, seq=s, tm=1024)
        a_out = _sgate(z, row(sg_ln_g[l]), row(sg_ln_b[l]), sg_w[l].astype(BF16), sg_b[l].T,
                       u_blk=u_col // sg_width, tm=512)
        b_out = _attn(z, attn_sink[l], seq=s, q_col=q_col, kv_col=kv_col, tq=512)
        x1, hf = _merge(a_out, b_out, z, x2, w_branch_a[l].astype(BF16), w_branch_b[l].astype(BF16),
                        w_out[l].astype(BF16), row(norm_mix_post[l]), row(norm_ffn_pre[l]), tm=512)
        act = _ffn_up(hf, w_up[l].astype(BF16), conv_w[l], row(conv_b[l]), seq=s, tm=1024, tn=512)
        x2 = _ffn_down(act, x1, w_down[l].astype(BF16), row(norm_ffn_post[l]), tm=256)
    return x2.reshape(b, s, d)
```

```python
import functools

import jax
import jax.numpy as jnp
from jax import lax
from jax.experimental import pallas as pl
from jax.experimental.pallas import tpu as pltpu

F32 = jnp.float32
BF16 = jnp.bfloat16

LANES = 128
F32_SUBLANES = 8
BF16_SUBLANES = 16
VMEM_BYTES_V7X = 64 * 1024 * 1024

SG_GROUPS = 8
SG_GROUP_DIM = 128
SG_CHUNK = 128
N_HEADS = 8
N_KV_HEADS = 2
HEAD_DIM = 128
Q_PER_KV = N_HEADS // N_KV_HEADS
WINDOW = 128
BLOCK = 128
ROPE_THETA = 500000.0
ROT_DIM = HEAD_DIM // 4
ROT_HALF = ROT_DIM // 2
CONV_WIDTH = 3
EPS = 1e-6
MASK_VALUE = -1e30

IN_TN = 512
IN_GROUP = 3
IN_ROWS = 256
MERGE_ROWS = 128
FFN_SUB = 256


def _vmem_limit(nbytes):
    return int(min(VMEM_BYTES_V7X - (4 << 20), nbytes))


def _rms(x, g):
    return x * lax.rsqrt(jnp.mean(x * x, axis=-1, keepdims=True) + EPS) * g


def _rope_table_kernel(invf_ref, c_ref, sa_ref, sb_ref):
    rows = c_ref.shape[0]
    pos = (lax.broadcasted_iota(jnp.int32, (rows, LANES), 0) + pl.program_id(0) * rows).astype(F32)
    ang = pos * invf_ref[...]
    lane = lax.broadcasted_iota(jnp.int32, (rows, LANES), 1)
    s = jnp.sin(ang)
    c_ref[...] = jnp.cos(ang)
    sa_ref[...] = jnp.where(lane < ROT_HALF, -s, 0.0)
    sb_ref[...] = jnp.where((lane >= ROT_HALF) & (lane < ROT_DIM), s, 0.0)


def _rope_tables(seq):
    inv_freq = ROPE_THETA ** (-jnp.arange(0, ROT_DIM, 2, dtype=F32) / ROT_DIM)
    invf = jnp.concatenate([inv_freq, inv_freq, jnp.zeros((LANES - ROT_DIM,), F32)]).reshape(1, LANES)
    rows = min(seq, 512)
    tab = jax.ShapeDtypeStruct((seq, LANES), F32)
    return pl.pallas_call(
        _rope_table_kernel,
        out_shape=(tab, tab, tab),
        grid=(seq // rows,),
        in_specs=[pl.BlockSpec((1, LANES), lambda i: (0, 0))],
        out_specs=tuple(pl.BlockSpec((rows, LANES), lambda i: (i, 0)) for _ in range(3)),
        name="rope_table",
    )(invf)


def _rope_head(z, c, sa, sb):
    return z * c + pltpu.roll(z, HEAD_DIM - ROT_HALF, 1) * sa + pltpu.roll(z, ROT_HALF, 1) * sb


def _in_proj_kernel(x_ref, g_ref, *refs, kinds):
    w_refs = refs[:IN_GROUP]
    c_ref, sa_ref, sb_ref, o_ref, h_ref, z_ref = refs[IN_GROUP:]
    j = pl.program_id(1)
    tm = h_ref.shape[0]
    chunks = [slice(r, r + IN_ROWS) for r in range(0, tm, IN_ROWS)]

    @pl.when(j == 0)
    def _():
        h_ref[...] = _rms(x_ref[...], g_ref[...]).astype(BF16)

    def epilogue(kind, rows, cols):
        if kind == "sigmoid":
            o_ref[rows, cols] = (0.5 * jnp.tanh(0.5 * z_ref[rows, cols]) + 0.5).astype(BF16)
        elif kind == "gelu":
            o_ref[rows, cols] = jax.nn.gelu(z_ref[rows, cols], approximate=True).astype(BF16)
        else:
            n_heads, scale = (IN_TN // HEAD_DIM, HEAD_DIM ** -0.5) if kind == "rope_q" else (N_KV_HEADS, None)
            c, sa, sb = c_ref[rows, :], sa_ref[rows, :], sb_ref[rows, :]
            for h in range(n_heads):
                sl = slice(cols.start + h * HEAD_DIM, cols.start + (h + 1) * HEAD_DIM)
                r = _rope_head(z_ref[rows, sl], c, sa, sb)
                o_ref[rows, sl] = (r if scale is None else r * scale).astype(BF16)
            if n_heads * HEAD_DIM < IN_TN:
                rest = slice(cols.start + n_heads * HEAD_DIM, cols.stop)
                o_ref[rows, rest] = z_ref[rows, rest].astype(BF16)

    n_steps = len(kinds) // IN_GROUP
    for step in range(n_steps):
        step_kinds = kinds[step * IN_GROUP:(step + 1) * IN_GROUP]
        if step > 0 and step_kinds == kinds[(step - 1) * IN_GROUP:step * IN_GROUP]:
            continue
        last = step
        while last + 1 < n_steps and kinds[(last + 1) * IN_GROUP:(last + 2) * IN_GROUP] == step_kinds:
            last += 1

        @pl.when((j >= step) & (j <= last))
        def _(step_kinds=step_kinds):
            for rows in chunks:
                for b in range(IN_GROUP):
                    z_ref[rows, b * IN_TN:(b + 1) * IN_TN] = jnp.dot(
                        h_ref[rows, :], w_refs[b][...], preferred_element_type=F32)
            for rows in chunks:
                for b in range(IN_GROUP):
                    epilogue(step_kinds[b], rows, slice(b * IN_TN, (b + 1) * IN_TN))


def _in_proj(x2, gain, w_bf, tabs, *, seq, tm):
    t, d = x2.shape
    n_in = w_bf.shape[1]
    sg_width = SG_GROUPS * SG_GROUP_DIM
    q_width = N_HEADS * HEAD_DIM
    kv_width = N_KV_HEADS * HEAD_DIM
    assert IN_TN == 2 * kv_width
    n_uv = 2 * sg_width // IN_TN
    n_q = q_width // IN_TN
    n_kv = 1
    n_gate = (n_in - 2 * sg_width - q_width - 2 * kv_width) // IN_TN
    nb = n_gate + n_uv + n_q + n_kv
    assert nb * IN_TN == n_in and nb % IN_GROUP == 0
    kinds = ("sigmoid",) * n_gate + ("gelu",) * n_uv + ("rope_q",) * n_q + ("rope_kv",) * n_kv
    first_gate_blk = n_uv + n_q + n_kv
    tiles_per_seq = seq // tm
    tab_spec = pl.BlockSpec((tm, LANES), lambda i, j: (i % tiles_per_seq, 0))
    w_specs = [pl.BlockSpec((d, IN_TN), lambda i, j, b=b: (0, (j * IN_GROUP + b + first_gate_blk) % nb))
               for b in range(IN_GROUP)]
    tn = IN_GROUP * IN_TN
    vmem = 2 * tm * d * 4 + tm * d * 2 + 2 * d * tn * 2 + 2 * tm * tn * 2 + 6 * tm * LANES * 4
    vmem += tm * tn * 4 + 8 * IN_ROWS * tn * 4
    return pl.pallas_call(
        functools.partial(_in_proj_kernel, kinds=kinds),
        out_shape=jax.ShapeDtypeStruct((t, n_in), BF16),
        grid=(t // tm, nb // IN_GROUP),
        in_specs=[
            pl.BlockSpec((tm, d), lambda i, j: (i, 0)),
            pl.BlockSpec((1, d), lambda i, j: (0, 0)),
            *w_specs,
            tab_spec, tab_spec, tab_spec,
        ],
        out_specs=pl.BlockSpec((tm, tn), lambda i, j: (i, j)),
        scratch_shapes=[pltpu.VMEM((tm, d), BF16), pltpu.VMEM((tm, tn), F32)],
        compiler_params=pltpu.CompilerParams(
            dimension_semantics=("parallel", "arbitrary"), vmem_limit_bytes=_vmem_limit(vmem)),
        name="in_proj",
    )(x2, gain, *([w_bf] * IN_GROUP), *tabs)


def _sgate_kernel(u_ref, v_ref, lng_ref, lnb_ref, w_ref, bt_ref, o_ref):
    tm = u_ref.shape[0]
    v = v_ref[...].astype(F32)
    mu = jnp.mean(v, axis=-1, keepdims=True)
    xc = v - mu
    vn = (xc * lax.rsqrt(jnp.mean(xc * xc, axis=-1, keepdims=True) + EPS) * lng_ref[...]
          + lnb_ref[...]).astype(BF16)
    for c in range(tm // SG_CHUNK):
        rows = slice(c * SG_CHUNK, (c + 1) * SG_CHUNK)
        for g in range(SG_GROUPS):
            cols = slice(g * SG_GROUP_DIM, (g + 1) * SG_GROUP_DIM)
            vz = jnp.dot(w_ref[g], vn[rows, cols], preferred_element_type=F32) + bt_ref[:, g:g + 1]
            o_ref[rows, cols] = (u_ref[rows, cols].astype(F32) * vz).astype(BF16)


def _sgate(z, ln_g, ln_b, w_bf, b_t, *, u_blk, tm):
    t = z.shape[0]
    sg_width = SG_GROUPS * SG_GROUP_DIM
    vmem = 4 * tm * sg_width * 2 + 2 * tm * sg_width * 2 + 4 * tm * sg_width * 4 + (1 << 20)
    return pl.pallas_call(
        _sgate_kernel,
        out_shape=jax.ShapeDtypeStruct((t, sg_width), BF16),
        grid=(t // tm,),
        in_specs=[
            pl.BlockSpec((tm, sg_width), lambda i: (i, u_blk)),
            pl.BlockSpec((tm, sg_width), lambda i: (i, u_blk + 1)),
            pl.BlockSpec((1, sg_width), lambda i: (0, 0)),
            pl.BlockSpec((1, sg_width), lambda i: (0, 0)),
            pl.BlockSpec((SG_GROUPS, SG_CHUNK, SG_CHUNK), lambda i: (0, 0, 0)),
            pl.BlockSpec((SG_CHUNK, SG_GROUPS), lambda i: (0, 0)),
        ],
        out_specs=pl.BlockSpec((tm, sg_width), lambda i: (i, 0)),
        compiler_params=pltpu.CompilerParams(
            dimension_semantics=("parallel",), vmem_limit_bytes=_vmem_limit(vmem)),
        name="sgate",
    )(z, z, ln_g, ln_b, w_bf, b_t)


def _attn_kernel(sink_ref, q_ref, kvp_ref, kvm_ref, kvn_ref, o_ref, k_ref, v_ref, s_ref, p_ref, *, seq):
    tq = q_ref.shape[0]
    kw = N_KV_HEADS * HEAD_DIM
    nrow = tq + 2 * BLOCK
    k_ref[0:BLOCK, :] = kvp_ref[:, 0:kw]
    k_ref[BLOCK:BLOCK + tq, :] = kvm_ref[:, 0:kw]
    k_ref[BLOCK + tq:, :] = kvn_ref[:, 0:kw]
    ones = jnp.ones((nrow, HEAD_DIM), BF16)
    for g in range(N_KV_HEADS):
        vc = slice(kw + g * HEAD_DIM, kw + (g + 1) * HEAD_DIM)
        v_ref[0:BLOCK, 2 * g * HEAD_DIM:(2 * g + 1) * HEAD_DIM] = kvp_ref[:, vc]
        v_ref[BLOCK:BLOCK + tq, 2 * g * HEAD_DIM:(2 * g + 1) * HEAD_DIM] = kvm_ref[:, vc]
        v_ref[BLOCK + tq:, 2 * g * HEAD_DIM:(2 * g + 1) * HEAD_DIM] = kvn_ref[:, vc]
        v_ref[:, (2 * g + 1) * HEAD_DIM:(2 * g + 2) * HEAD_DIM] = ones
    blk0 = (pl.program_id(0) * tq) % seq // BLOCK
    nq = Q_PER_KV * BLOCK
    qi = lax.broadcasted_iota(jnp.int32, (nq, 3 * BLOCK), 0) & (BLOCK - 1)
    kj = lax.broadcasted_iota(jnp.int32, (nq, 3 * BLOCK), 1)
    rel = kj - BLOCK - qi
    band = (rel >= -WINDOW) & (rel <= WINDOW)
    units = [(qb, g) for qb in range(tq // BLOCK) for g in range(N_KV_HEADS)]

    def heads_of(g):
        return [g * Q_PER_KV + i for i in range(Q_PER_KV)]

    def scores(u):
        qb, g = units[u]
        rows = slice(qb * BLOCK, (qb + 1) * BLOCK)
        qg = jnp.concatenate([q_ref[rows, h * HEAD_DIM:(h + 1) * HEAD_DIM] for h in heads_of(g)], axis=0)
        kwin = k_ref[qb * BLOCK:(qb + 3) * BLOCK, g * HEAD_DIM:(g + 1) * HEAD_DIM]
        s_ref[u % 3] = lax.dot_general(qg, kwin, (((1,), (1,)), ((), ())), preferred_element_type=F32)

    def softmax_num(u):
        qb, g = units[u]
        k_pos = (blk0 + qb - 1) * BLOCK + kj
        valid = band & (k_pos >= 0) & (k_pos < seq)
        s = jnp.where(valid, s_ref[u % 3], MASK_VALUE)
        sink = jnp.concatenate([jnp.full((BLOCK, 1), sink_ref[h], F32) for h in heads_of(g)], axis=0)
        m = jnp.maximum(jnp.max(s, axis=-1, keepdims=True), sink)
        p_ref[u % 2] = jnp.exp(s - m).astype(BF16)
        return jnp.exp(sink - m)

    def weighted_sum(u, e_sink):
        qb, g = units[u]
        rows = slice(qb * BLOCK, (qb + 1) * BLOCK)
        vwin = v_ref[qb * BLOCK:(qb + 3) * BLOCK, 2 * g * HEAD_DIM:(2 * g + 2) * HEAD_DIM]
        ox = jnp.dot(p_ref[u % 2], vwin, preferred_element_type=F32)
        o = ox[:, 0:HEAD_DIM] / (ox[:, HEAD_DIM:] + e_sink)
        for i, h in enumerate(heads_of(g)):
            o_ref[rows, h * HEAD_DIM:(h + 1) * HEAD_DIM] = o[i * BLOCK:(i + 1) * BLOCK].astype(BF16)

    scores(0)
    scores(1)
    for u in range(len(units)):
        e_sink = softmax_num(u)
        if u + 2 < len(units):
            scores(u + 2)
        weighted_sum(u, e_sink)


def _attn(z, sink, *, seq, q_col, kv_col, tq):
    t = z.shape[0]
    q_width = N_HEADS * HEAD_DIM
    kvw = 2 * N_KV_HEADS * HEAD_DIM
    assert q_col % q_width == 0 and kv_col % kvw == 0
    qb_, kb_ = q_col // q_width, kv_col // kvw
    r = tq // BLOCK
    nblk = t // BLOCK
    vmem = 4 * tq * q_width * 2 + 2 * (tq + 2 * BLOCK) * kvw * 2 + (tq + 2 * BLOCK) * kvw * 2
    vmem += 16 * Q_PER_KV * BLOCK * 3 * BLOCK * 4
    return pl.pallas_call(
        functools.partial(_attn_kernel, seq=seq),
        out_shape=jax.ShapeDtypeStruct((t, q_width), BF16),
        grid=(t // tq,),
        in_specs=[
            pl.BlockSpec(memory_space=pltpu.SMEM),
            pl.BlockSpec((tq, q_width), lambda i: (i, qb_)),
            pl.BlockSpec((BLOCK, kvw), lambda i: (jnp.maximum(i * r - 1, 0), kb_)),
            pl.BlockSpec((tq, kvw), lambda i: (i, kb_)),
            pl.BlockSpec((BLOCK, kvw), lambda i: (jnp.minimum((i + 1) * r, nblk - 1), kb_)),
        ],
        out_specs=pl.BlockSpec((tq, q_width), lambda i: (i, 0)),
        scratch_shapes=[pltpu.VMEM((tq + 2 * BLOCK, kvw // 2), BF16),
                        pltpu.VMEM((tq + 2 * BLOCK, kvw), BF16),
                        pltpu.VMEM((3, Q_PER_KV * BLOCK, 3 * BLOCK), F32),
                        pltpu.VMEM((2, Q_PER_KV * BLOCK, 3 * BLOCK), BF16)],
        compiler_params=pltpu.CompilerParams(
            dimension_semantics=("parallel",), vmem_limit_bytes=_vmem_limit(vmem)),
        name="attn",
    )(sink, z, z, z, z)


def _merge_kernel(a_ref, b_ref, ga_ref, gb_ref, x_ref, wa_ref, wb_ref, wo_ref, gpost_ref, gpre_ref,
                  x1_ref, hf_ref, m_ref, y_ref):
    tm = a_ref.shape[0]
    chunks = [slice(r, r + MERGE_ROWS) for r in range(0, tm, MERGE_ROWS)]
    for rows in chunks:
        pa = jnp.dot(a_ref[rows, :], wa_ref[...], preferred_element_type=F32)
        pb = jnp.dot(b_ref[rows, :], wb_ref[...], preferred_element_type=F32)
        m_ref[rows, :] = (ga_ref[rows, :].astype(F32) * pa + gb_ref[rows, :].astype(F32) * pb).astype(BF16)
    for rows in chunks:
        y_ref[rows, :] = jnp.dot(m_ref[rows, :], wo_ref[...], preferred_element_type=F32)
    for rows in chunks:
        x1 = x_ref[rows, :] + _rms(y_ref[rows, :], gpost_ref[...])
        x1_ref[rows, :] = x1
        hf_ref[rows, :] = _rms(x1, gpre_ref[...]).astype(BF16)


def _merge(a, b, z, x2, wa, wb, wo, gpost, gpre, *, tm):
    t, d = x2.shape
    ka, kb = a.shape[1], b.shape[1]
    resident = dict(pipeline_mode=pl.Buffered(1))
    vmem = 2 * tm * (ka + kb) * 2 + 4 * tm * d * 2 + 2 * tm * d * 4 + (ka + kb + d) * d * 2
    vmem += 2 * tm * d * 4 + 2 * tm * d * 2 + 6 * tm * d * 4
    return pl.pallas_call(
        _merge_kernel,
        out_shape=(jax.ShapeDtypeStruct((t, d), F32), jax.ShapeDtypeStruct((t, d), BF16)),
        grid=(t // tm,),
        in_specs=[
            pl.BlockSpec((tm, ka), lambda i: (i, 0)),
            pl.BlockSpec((tm, kb), lambda i: (i, 0)),
            pl.BlockSpec((tm, d), lambda i: (i, 0)),
            pl.BlockSpec((tm, d), lambda i: (i, 1)),
            pl.BlockSpec((tm, d), lambda i: (i, 0)),
            pl.BlockSpec((ka, d), lambda i: (0, 0), **resident),
            pl.BlockSpec((kb, d), lambda i: (0, 0), **resident),
            pl.BlockSpec((d, d), lambda i: (0, 0), **resident),
            pl.BlockSpec((1, d), lambda i: (0, 0)),
            pl.BlockSpec((1, d), lambda i: (0, 0)),
        ],
        out_specs=(pl.BlockSpec((tm, d), lambda i: (i, 0)), pl.BlockSpec((tm, d), lambda i: (i, 0))),
        scratch_shapes=[pltpu.VMEM((tm, d), BF16), pltpu.VMEM((tm, d), F32)],
        compiler_params=pltpu.CompilerParams(
            dimension_semantics=("parallel",), vmem_limit_bytes=_vmem_limit(vmem)),
        name="merge",
    )(a, b, z, z, x2, wa, wb, wo, gpost, gpre)


def _ffn_up_kernel(hp_ref, hm_ref, hn_ref, wg_ref, wv_ref, cwg_ref, cwv_ref, cbg_ref, cbv_ref, o_ref,
                   src_ref, hx_ref, ug_ref, uv_ref, actf_ref, *, seq):
    i, j = pl.program_id(0), pl.program_id(1)
    tm = hm_ref.shape[0]
    ext = hx_ref.shape[0]
    pitch = ext // F32_SUBLANES
    hal = F32_SUBLANES
    n_slab_in = hm_ref.shape[1] // LANES

    @pl.when(j == 0)
    def _():
        first = (i * tm) % seq == 0
        last = ((i + 1) * tm) % seq == 0
        hp = hp_ref[hp_ref.shape[0] - hal:, :].astype(F32)
        hn = hn_ref[0:hal, :].astype(F32)
        hp = jnp.where(first, jnp.zeros_like(hp), hp)
        hn = jnp.where(last, jnp.zeros_like(hn), hn)
        for l in range(n_slab_in):
            lanes = slice(l * LANES, (l + 1) * LANES)
            src_ref[0:hal, :] = hp[:, lanes]
            src_ref[hal:hal + tm, :] = hm_ref[:, lanes].astype(F32)
            src_ref[hal + tm:, :] = hn[:, lanes]

            def interleave(w, carry, lanes=lanes):
                r0 = pl.multiple_of(w * BF16_SUBLANES, BF16_SUBLANES)
                a = src_ref[pl.ds(2 * w, F32_SUBLANES, stride=pitch), :]
                b = src_ref[pl.ds(2 * w + 1, F32_SUBLANES, stride=pitch), :]
                hx_ref[pl.ds(r0, BF16_SUBLANES), lanes] = jnp.concatenate([a, b], axis=0).astype(BF16)
                return carry

            lax.fori_loop(0, ext // BF16_SUBLANES, interleave, 0, unroll=5)

    def conv(u, cw, cb):
        body = ext - F32_SUBLANES
        prev = jnp.concatenate([pltpu.roll(u[body:], 1, 0), u[:body]], axis=0)
        nxt = jnp.concatenate([u[F32_SUBLANES:], pltpu.roll(u[:F32_SUBLANES], F32_SUBLANES - 1, 0)], axis=0)
        return cb + cw[0:1] * prev + cw[1:2] * u + cw[2:3] * nxt

    tn = wg_ref.shape[1]
    subs = [slice(c * FFN_SUB, (c + 1) * FFN_SUB) for c in range(tn // FFN_SUB)]
    for cols in subs:
        ug_ref[:, cols] = jnp.dot(hx_ref[...], wg_ref[:, cols].astype(BF16), preferred_element_type=F32)
        uv_ref[:, cols] = jnp.dot(hx_ref[...], wv_ref[:, cols].astype(BF16), preferred_element_type=F32)
    for c, cols in enumerate(subs):
        act = (jax.nn.gelu(conv(ug_ref[:, cols], cwg_ref[:, cols], cbg_ref[:, cols]), approximate=True)
               * conv(uv_ref[:, cols], cwv_ref[:, cols], cbv_ref[:, cols]))
        for l in range(FFN_SUB // LANES):
            slab = c * (FFN_SUB // LANES) + l
            for v in range(pitch):
                actf_ref[slab, pl.ds(v, F32_SUBLANES, stride=pitch), :] = (
                    act[v * F32_SUBLANES:(v + 1) * F32_SUBLANES, l * LANES:(l + 1) * LANES])
            o_ref[:, slab * LANES:(slab + 1) * LANES] = actf_ref[slab, hal:hal + tm, :].astype(BF16)


def _ffn_up(hf, w_up, conv_w, conv_b, *, seq, tm, tn):
    t, d = hf.shape
    dff = w_up.shape[1] // 2
    assert dff % tn == 0 and tn % FFN_SUB == 0
    nj = dff // tn
    halo = BF16_SUBLANES
    r = tm // halo
    nhb = t // halo
    ext = tm + 2 * F32_SUBLANES
    assert ext % BF16_SUBLANES == 0 and d % LANES == 0
    vmem = 2 * (tm + 2 * halo) * d * 2 + ext * LANES * 4 + ext * d * 2 + 2 * 2 * d * tn * w_up.dtype.itemsize
    vmem += 2 * tm * tn * 2 + 3 * ext * tn * 4 + 8 * ext * FFN_SUB * 4
    return pl.pallas_call(
        functools.partial(_ffn_up_kernel, seq=seq),
        out_shape=jax.ShapeDtypeStruct((t, dff), BF16),
        grid=(t // tm, nj),
        in_specs=[
            pl.BlockSpec((halo, d), lambda i, j: (jnp.maximum(i * r - 1, 0), 0)),
            pl.BlockSpec((tm, d), lambda i, j: (i, 0)),
            pl.BlockSpec((halo, d), lambda i, j: (jnp.minimum((i + 1) * r, nhb - 1), 0)),
            pl.BlockSpec((d, tn), lambda i, j: (0, j)),
            pl.BlockSpec((d, tn), lambda i, j: (0, j + nj)),
            pl.BlockSpec((CONV_WIDTH, tn), lambda i, j: (0, j)),
            pl.BlockSpec((CONV_WIDTH, tn), lambda i, j: (0, j + nj)),
            pl.BlockSpec((1, tn), lambda i, j: (0, j)),
            pl.BlockSpec((1, tn), lambda i, j: (0, j + nj)),
        ],
        out_specs=pl.BlockSpec((tm, tn), lambda i, j: (i, j)),
        scratch_shapes=[pltpu.VMEM((ext, LANES), F32), pltpu.VMEM((ext, d), BF16),
                        pltpu.VMEM((ext, tn), F32), pltpu.VMEM((ext, tn), F32),
                        pltpu.VMEM((tn // LANES, ext, LANES), F32)],
        compiler_params=pltpu.CompilerParams(
            dimension_semantics=("parallel", "arbitrary"), vmem_limit_bytes=_vmem_limit(vmem)),
        name="ffn_up",
    )(hf, hf, hf, w_up, w_up, conv_w, conv_w, conv_b, conv_b)


def _ffn_down_kernel(a_ref, wd_ref, x1_ref, gpost_ref, o_ref, f_ref):
    i = pl.program_id(0)
    n = pl.num_programs(0) - 1

    @pl.when(i == 0)
    def _():
        f_ref[...] = jnp.zeros_like(f_ref)

    def finish():
        o_ref[...] = x1_ref[...] + _rms(f_ref[...], gpost_ref[...])

    @pl.when(i < n)
    def _():
        f_new = jnp.dot(a_ref[...], wd_ref[...], preferred_element_type=F32)
        finish()
        f_ref[...] = f_new

    @pl.when(i == n)
    def _():
        finish()


def _ffn_down(act, x1, w_down, gpost, *, tm):
    t, d = x1.shape
    dff = w_down.shape[0]
    n = t // tm
    prev = lambda i: (jnp.maximum(i - 1, 0), 0)
    vmem = dff * d * 2 + 2 * tm * dff * 2 + 4 * tm * d * 4 + tm * d * 4 + 8 * tm * d * 4
    return pl.pallas_call(
        _ffn_down_kernel,
        out_shape=jax.ShapeDtypeStruct((t, d), F32),
        grid=(n + 1,),
        in_specs=[
            pl.BlockSpec((tm, dff), lambda i: (jnp.minimum(i, n - 1), 0)),
            pl.BlockSpec((dff, d), lambda i: (0, 0), pipeline_mode=pl.Buffered(1)),
            pl.BlockSpec((tm, d), prev),
            pl.BlockSpec((1, d), lambda i: (0, 0)),
        ],
        out_specs=pl.BlockSpec((tm, d), prev),
        scratch_shapes=[pltpu.VMEM((tm, d), F32)],
        compiler_params=pltpu.CompilerParams(
            dimension_semantics=("arbitrary",), vmem_limit_bytes=_vmem_limit(vmem)),
        name="ffn_down",
    )(act, w_down, x1, gpost)


def kernel(x, norm_mix_pre, w_in, sg_ln_g, sg_ln_b, sg_w, sg_b, attn_sink, w_branch_a, w_branch_b, w_out,
           norm_mix_post, norm_ffn_pre, w_up, conv_w, conv_b, w_down, norm_ffn_post):
    b, s, d = x.shape
    t = b * s
    depth = w_in.shape[0]
    sg_width = SG_GROUPS * SG_GROUP_DIM
    n_in = w_in.shape[-1]
    gate_width = n_in - 2 * sg_width - N_HEADS * HEAD_DIM - 2 * N_KV_HEADS * HEAD_DIM
    assert gate_width == 2 * d
    u_col = gate_width
    q_col = u_col + 2 * sg_width
    kv_col = q_col + N_HEADS * HEAD_DIM
    tabs = _rope_tables(s)
    row = lambda p: p.reshape(1, -1)
    x2 = x.reshape(t, d)
    for l in range(depth):
        z = _in_proj(x2, row(norm_mix_pre[l]), w_in[l].astype(BF16), tabs, seq=s, tm=1024)
        a_out = _sgate(z, row(sg_ln_g[l]), row(sg_ln_b[l]), sg_w[l].astype(BF16), sg_b[l].T,
                       u_blk=u_col // sg_width, tm=512)
        b_out = _attn(z, attn_sink[l], seq=s, q_col=q_col, kv_col=kv_col, tq=512)
        x1, hf = _merge(a_out, b_out, z, x2, w_branch_a[l].astype(BF16), w_branch_b[l].astype(BF16),
                        w_out[l].astype(BF16), row(norm_mix_post[l]), row(norm_ffn_pre[l]), tm=512)
        act = _ffn_up(hf, w_up[l], conv_w[l], row(conv_b[l]), seq=s, tm=1024, tn=512)
        x2 = _ffn_down(act, x1, w_down[l].astype(BF16), row(norm_ffn_post[l]), tm=256)
    return x2.reshape(b, s, d)
```
